```python
import math
import jax, jax.numpy as jnp
from jax import lax
import numpy as np

D_MODEL = 1024
BATCH = 16
SEQ = 4096
DEPTH = 4

CHUNK = 64
N_META = 16
N_MIXERS = 3
POOL_WINDOWS = (2, 4, 8, 16)
N_POOL_GROUPS = len(POOL_WINDOWS)
POOL_GROUP = D_MODEL // N_POOL_GROUPS
N_HEADS = 8
HEAD_DIM = D_MODEL // N_HEADS
KV_LATENT = D_MODEL // 4
IDX_HEADS = 8
IDX_DIM = 64
TOPK_MAX = 256
Q_BLOCK = 128
CONV_WIDTH = 3
D_FF = -(-8 * D_MODEL // (3 * 256)) * 256
B_PROJ = D_MODEL + KV_LATENT + IDX_HEADS * IDX_DIM + IDX_DIM + IDX_HEADS
ALPHA = (2.0 * DEPTH) ** 0.25
BETA = (8.0 * DEPTH) ** -0.25
LN_EPS = 1e-5

kernel_name = "hybrid_pool_dsa_shortconv_deepnorm_trunk"


def _n_of_kind(kind):
    return (DEPTH - kind + N_MIXERS - 1) // N_MIXERS


def _chunk_ids(t):
    pos = jnp.arange(t)
    return jnp.where(pos < N_META, 0, 1 + (pos - N_META) // CHUNK)


def layer_norm(x, g, b):
    xf = x.astype(jnp.float32)
    mu = jnp.mean(xf, axis=-1, keepdims=True)
    var = jnp.mean(jnp.square(xf - mu), axis=-1, keepdims=True)
    y = (xf - mu) * lax.rsqrt(var + LN_EPS) * g.astype(jnp.float32) + b.astype(jnp.float32)
    return y.astype(x.dtype)


def swiglu_ffn(x, w_gu, w_down):
    gate, up = jnp.split(x @ w_gu, 2, axis=-1)
    return (jax.nn.silu(gate) * up) @ w_down


def pool_mixer(x, w_in, w_group, scale, w_out):
    b, t, _ = x.shape
    u = (x @ w_in).reshape(b, t, N_POOL_GROUPS, POOL_GROUP)
    cs = jnp.cumsum(u.astype(jnp.float32), axis=1)
    cs = jnp.pad(cs, ((0, 0), (1, 0), (0, 0), (0, 0)))
    hi = jnp.arange(t) + 1
    pooled = []
    for g, w in enumerate(POOL_WINDOWS):
        lo = jnp.maximum(hi - w, 0)
        win = cs[:, hi, g] - cs[:, lo, g]
        pooled.append(win / (hi - lo).astype(jnp.float32)[None, :, None])
    pooled = jnp.stack(pooled, axis=2).astype(x.dtype)
    y = jnp.einsum('btgc,gce->btge', pooled - u, w_group).reshape(b, t, D_MODEL)
    return (y * scale) @ w_out


def short_conv_mixer(x, w_in, conv_w, w_out):
    t = x.shape[1]
    bg, cg, hv = jnp.split(x @ w_in, 3, axis=-1)
    z = jnp.pad(cg * hv, ((0, 0), (CONV_WIDTH - 1, 0), (0, 0)))
    conv = sum(z[:, k:k + t] * conv_w[k] for k in range(CONV_WIDTH))
    return (bg * conv) @ w_out


def dsa_mixer(x, n_keys, w_in, w_uk, w_uv, w_out):
    b, t, _ = x.shape
    topk = min(TOPK_MAX, n_keys // 4)
    proj = x @ w_in
    o = 0
    q = proj[..., o:o + D_MODEL].reshape(b, t, N_HEADS, HEAD_DIM); o += D_MODEL
    c_kv = proj[..., o:o + KV_LATENT]; o += KV_LATENT
    q_idx = proj[..., o:o + IDX_HEADS * IDX_DIM].reshape(b, t, IDX_HEADS, IDX_DIM); o += IDX_HEADS * IDX_DIM
    k_idx = proj[..., o:o + IDX_DIM].astype(jnp.float32); o += IDX_DIM
    w_idx = proj[..., o:o + IDX_HEADS]

    nblk = -(-t // Q_BLOCK)
    tp = nblk * Q_BLOCK
    cid_all = _chunk_ids(tp)
    cid_k = cid_all[:t]

    def to_blocks(a):
        a = jnp.pad(a, [(0, 0), (0, tp - t)] + [(0, 0)] * (a.ndim - 2))
        return jnp.moveaxis(a.reshape(b, nblk, Q_BLOCK, *a.shape[2:]), 1, 0)

    def attend_block(args):
        qb, qib, wib, cq = args
        s = jnp.einsum('bqhd,bsd->bqhs', qib.astype(jnp.float32), k_idx)
        score = jnp.einsum('bqhs,bqh->bqs', jax.nn.relu(s), wib.astype(jnp.float32))
        allowed = cid_k[None, :] <= cq[:, None]
        score = jnp.where(allowed[None], score, -jnp.inf)
        top_val, top_idx = lax.top_k(score, topk)
        valid = top_val > -jnp.inf
        kv_sel = jax.vmap(lambda c, i: c[i])(c_kv, top_idx)
        q_lat = jnp.einsum('bqhd,hdl->bqhl', qb, w_uk)
        logits = jnp.einsum('bqhl,bqkl->bqhk', q_lat, kv_sel).astype(jnp.float32)
        logits = jnp.where(valid[:, :, None, :], logits * (HEAD_DIM ** -0.5), -jnp.inf)
        p = jax.nn.softmax(logits, axis=-1).astype(kv_sel.dtype)
        return jnp.einsum('bqhk,bqkl->bqhl', p, kv_sel)

    ctx = lax.map(attend_block, (to_blocks(q), to_blocks(q_idx), to_blocks(w_idx),
                                 cid_all.reshape(nblk, Q_BLOCK)))
    ctx = jnp.moveaxis(ctx, 0, 1).reshape(b, tp, N_HEADS, KV_LATENT)[:, :t]
    out = jnp.einsum('bthl,hld->bthd', ctx, w_uv).reshape(b, t, D_MODEL)
    return out @ w_out


def setup_inputs(seed: int = 0) -> dict:
    key = jax.random.key(seed)
    ks = iter(jax.random.split(key, 32))
    n_a, n_b, n_c = _n_of_kind(0), _n_of_kind(1), _n_of_kind(2)

    def nrm(shape, scale):
        return jax.random.normal(next(ks), shape, jnp.float32) * scale

    def near_one(shape):
        return 1.0 + nrm(shape, 0.02)

    d = D_MODEL
    return {
        "x": nrm((BATCH, SEQ, d), 1.0),
        "meta": nrm((N_META, d), 1.0),
        "a_w_in": nrm((n_a, d, d), d ** -0.5),
        "a_w_group": nrm((n_a, N_POOL_GROUPS, POOL_GROUP, POOL_GROUP), POOL_GROUP ** -0.5),
        "a_scale": near_one((n_a, d)),
        "a_w_out": nrm((n_a, d, d), BETA * d ** -0.5),
        "b_w_in": nrm((n_b, d, B_PROJ), d ** -0.5),
        "b_w_uk": nrm((n_b, N_HEADS, HEAD_DIM, KV_LATENT), HEAD_DIM ** -0.5),
        "b_w_uv": nrm((n_b, N_HEADS, KV_LATENT, HEAD_DIM), KV_LATENT ** -0.5),
        "b_w_out": nrm((n_b, d, d), BETA * d ** -0.5),
        "c_w_in": nrm((n_c, d, 3 * d), d ** -0.5),
        "c_conv": nrm((n_c, CONV_WIDTH, d), CONV_WIDTH ** -0.5),
        "c_w_out": nrm((n_c, d, d), BETA * d ** -0.5),
        "ln_mix_g": near_one((DEPTH, d)),
        "ln_mix_b": nrm((DEPTH, d), 0.02),
        "ffn_w_gu": nrm((DEPTH, d, 2 * D_FF), d ** -0.5),
        "ffn_w_down": nrm((DEPTH, D_FF, d), BETA * D_FF ** -0.5),
        "ln_ffn_g": near_one((DEPTH, d)),
        "ln_ffn_b": nrm((DEPTH, d), 0.02),
    }


def reference(x, meta, a_w_in, a_w_group, a_scale, a_w_out, b_w_in, b_w_uk, b_w_uv,
              b_w_out, c_w_in, c_conv, c_w_out, ln_mix_g, ln_mix_b, ffn_w_gu,
              ffn_w_down, ln_ffn_g, ln_ffn_b):
    b, seq, d = x.shape
    h = jnp.concatenate([jnp.broadcast_to(meta[None].astype(x.dtype), (b, N_META, d)), x], axis=1)
    for i in range(DEPTH):
        kind, j = i % N_MIXERS, i // N_MIXERS
        if kind == 0:
            m = pool_mixer(h, a_w_in[j], a_w_group[j], a_scale[j], a_w_out[j])
        elif kind == 1:
            m = dsa_mixer(h, seq, b_w_in[j], b_w_uk[j], b_w_uv[j], b_w_out[j])
        else:
            m = short_conv_mixer(h, c_w_in[j], c_conv[j], c_w_out[j])
        h = layer_norm(ALPHA * h + m, ln_mix_g[i], ln_mix_b[i])
        h = layer_norm(ALPHA * h + swiglu_ffn(h, ffn_w_gu[i], ffn_w_down[i]), ln_ffn_g[i], ln_ffn_b[i])
    return h[:, N_META:]
```

```python
import functools

import jax
import jax.numpy as jnp
from jax import lax
from jax.experimental import pallas as pl
from jax.experimental.pallas import tpu as pltpu

CHUNK = 64
N_META = 16
POOL_WINDOWS = (2, 4, 8, 16)
N_HEADS = 8
KV_LATENT_DIV = 4
IDX_HEADS = 8
IDX_DIM = 64
TOPK_MAX = 256
CONV_WIDTH = 3
DEPTH = 4
ALPHA = (2.0 * DEPTH) ** 0.25
LN_EPS = 1e-5

BF16 = jnp.bfloat16
F32 = jnp.float32
I32 = jnp.int32

VMEM_LIMIT_BYTES = 56 * 1024 * 1024
DSA_TILE = 256
INT_MIN = -(2 ** 31)
KEY_NEG_INF = -2139095041
NEG_INF = float("-inf")
MAX_FLOOR = -1e30


def _cparams(n_axes):
    return pltpu.CompilerParams(
        dimension_semantics=("arbitrary",) * n_axes,
        vmem_limit_bytes=VMEM_LIMIT_BYTES,
    )


def _const_spec(shape):
    nd = len(shape)
    return pl.BlockSpec(shape, lambda *_: (0,) * nd, pipeline_mode=pl.Buffered(1))


def _layer_norm(z, g, b):
    mu = jnp.mean(z, axis=-1, keepdims=True)
    zc = z - mu
    var = jnp.mean(zc * zc, axis=-1, keepdims=True)
    return zc * lax.rsqrt(var + LN_EPS) * g + b


def _dot(a, b):
    return jnp.dot(a, b, preferred_element_type=F32)


def _dot_nt(a, b):
    return lax.dot_general(a, b, (((1,), (1,)), ((), ())), preferred_element_type=F32)


def _ffn_kernel(h_ref, wgu_ref, wd_ref, g_ref, b_ref, o_ref, *, d_ff, ff_chunk):
    h = h_ref[...]
    hb = h.astype(BF16)
    acc = ALPHA * h
    for c in range(d_ff // ff_chunk):
        lo = c * ff_chunk
        gate = _dot(hb, wgu_ref[:, lo:lo + ff_chunk])
        up = _dot(hb, wgu_ref[:, d_ff + lo:d_ff + lo + ff_chunk])
        act = (gate * jax.nn.sigmoid(gate) * up).astype(BF16)
        acc = acc + _dot(act, wd_ref[lo:lo + ff_chunk, :])
    o_ref[...] = _layer_norm(acc, g_ref[...], b_ref[...])


def _ffn(h2d, w_gu, w_down, g, b, *, tile):
    n, d = h2d.shape
    d_ff = w_down.shape[0]
    ff_chunk = 256 if d_ff % 256 == 0 else d_ff
    return pl.pallas_call(
        functools.partial(_ffn_kernel, d_ff=d_ff, ff_chunk=ff_chunk),
        grid=(n // tile,),
        in_specs=[
            pl.BlockSpec((tile, d), lambda i: (i, 0)),
            _const_spec(w_gu.shape),
            _const_spec(w_down.shape),
            _const_spec(g.shape),
            _const_spec(b.shape),
        ],
        out_specs=pl.BlockSpec((tile, d), lambda i: (i, 0)),
        out_shape=jax.ShapeDtypeStruct((n, d), F32),
        compiler_params=_cparams(1),
        name="ffn_ln",
    )(h2d, w_gu, w_down, g, b)


POOL_HALO = 16


def _pool_kernel(x_ref, meta_ref, win_ref, wg_ref, sc_ref, wout_ref, g_ref, b_ref,
                 o_ref, ubuf, *, tq, is_meta):
    j = pl.program_id(1)
    d = x_ref.shape[-1]
    gw = d // len(POOL_WINDOWS)
    x = x_ref[...]
    u = _dot(x.astype(BF16), win_ref[...])

    if is_meta:
        ubuf[0:POOL_HALO, :] = jnp.zeros((POOL_HALO, d), F32)
    else:
        @pl.when(j == 0)
        def _():
            ubuf[0:POOL_HALO, :] = _dot(meta_ref[...].astype(BF16), win_ref[...])

        @pl.when(j > 0)
        def _():
            ubuf[0:POOL_HALO, :] = ubuf[tq:tq + POOL_HALO, :]
    ubuf[POOL_HALO:POOL_HALO + tq, :] = u

    ys = []
    for gi, w in enumerate(POOL_WINDOWS):
        c0 = gi * gw
        s = u[:, c0:c0 + gw]
        for k in range(1, w):
            s = s + ubuf[POOL_HALO - k:POOL_HALO - k + tq, c0:c0 + gw]
        if is_meta:
            pos = lax.broadcasted_iota(I32, (tq, 1), 0) + 1
            cnt = jnp.minimum(pos, w).astype(F32)
            pooled = s / cnt
        else:
            pooled = s / float(w)
        dlt = (pooled - u[:, c0:c0 + gw]).astype(BF16)
        ys.append(_dot(dlt, wg_ref[gi]))
    y = jnp.concatenate(ys, axis=-1) * sc_ref[...]
    m = _dot(y.astype(BF16), wout_ref[...])
    o_ref[...] = _layer_norm(ALPHA * x + m, g_ref[...], b_ref[...])


def _pool_layer(x3, meta, w_in, w_group, scale, w_out, g, b, *, tq, is_meta):
    bsz, t, d = x3.shape
    return pl.pallas_call(
        functools.partial(_pool_kernel, tq=tq, is_meta=is_meta),
        grid=(bsz, t // tq),
        in_specs=[
            pl.BlockSpec((None, tq, d), lambda i, j: (i, j, 0)),
            _const_spec(meta.shape),
            _const_spec(w_in.shape),
            _const_spec(w_group.shape),
            _const_spec(scale.shape),
            _const_spec(w_out.shape),
            _const_spec(g.shape),
            _const_spec(b.shape),
        ],
        out_specs=pl.BlockSpec((None, tq, d), lambda i, j: (i, j, 0)),
        out_shape=jax.ShapeDtypeStruct((bsz, t, d), F32),
        scratch_shapes=[pltpu.VMEM((tq + POOL_HALO, d), F32)],
        compiler_params=_cparams(2),
        name="pool_ln",
    )(x3, meta, w_in, w_group, scale, w_out, g, b)


CONV_HALO = 8


def _conv_kernel(x_ref, meta_ref, win_ref, cw_ref, wout_ref, g_ref, b_ref,
                 o_ref, zbuf, *, tq, is_meta):
    j = pl.program_id(1)
    d = x_ref.shape[-1]
    x = x_ref[...]
    xb = x.astype(BF16)

    def gates(vb):
        bg = _dot(vb, win_ref[:, 0:d])
        cg = _dot(vb, win_ref[:, d:2 * d])
        hv = _dot(vb, win_ref[:, 2 * d:3 * d])
        return bg, cg * hv

    bg, z = gates(xb)
    if is_meta:
        zbuf[0:CONV_HALO, :] = jnp.zeros((CONV_HALO, d), F32)
    else:
        @pl.when(j == 0)
        def _():
            _, zm = gates(meta_ref[...].astype(BF16))
            zbuf[0:CONV_HALO, :] = zm[N_META - CONV_HALO:N_META, :]

        @pl.when(j > 0)
        def _():
            zbuf[0:CONV_HALO, :] = zbuf[tq:tq + CONV_HALO, :]
    zbuf[CONV_HALO:CONV_HALO + tq, :] = z

    conv = z * cw_ref[CONV_WIDTH - 1:CONV_WIDTH, :]
    for k in range(CONV_WIDTH - 1):
        sh = CONV_WIDTH - 1 - k
        conv = conv + zbuf[CONV_HALO - sh:CONV_HALO - sh + tq, :] * cw_ref[k:k + 1, :]
    m = _dot((bg * conv).astype(BF16), wout_ref[...])
    o_ref[...] = _layer_norm(ALPHA * x + m, g_ref[...], b_ref[...])


def _conv_layer(x3, meta, w_in, conv_w, w_out, g, b, *, tq, is_meta):
    bsz, t, d = x3.shape
    return pl.pallas_call(
        functools.partial(_conv_kernel, tq=tq, is_meta=is_meta),
        grid=(bsz, t // tq),
        in_specs=[
            pl.BlockSpec((None, tq, d), lambda i, j: (i, j, 0)),
            _const_spec(meta.shape),
            _const_spec(w_in.shape),
            _const_spec(conv_w.shape),
            _const_spec(w_out.shape),
            _const_spec(g.shape),
            _const_spec(b.shape),
        ],
        out_specs=pl.BlockSpec((None, tq, d), lambda i, j: (i, j, 0)),
        out_shape=jax.ShapeDtypeStruct((bsz, t, d), F32),
        scratch_shapes=[pltpu.VMEM((tq + CONV_HALO, d), F32)],
        compiler_params=_cparams(2),
        name="conv_ln",
    )(x3, meta, w_in, conv_w, w_out, g, b)


def _split_hi_lo(v):
    hi = v.astype(BF16)
    lo = (v - hi.astype(F32)).astype(BF16)
    return hi, lo


def _dsa_proj_kernel(x_ref, wq_ref, wukt_ref, wckv_ref, wqi3t_ref, wk3_ref, wwt_ref,
                     qlatT_ref, ckv_ref, ckvT_ref, qiT_ref, kcat_ref, wT_ref):
    xb = x_ref[...].astype(BF16)
    hd = wukt_ref.shape[2]
    q = _dot(xb, wq_ref[...])
    for h in range(N_HEADS):
        qh = q[:, h * hd:(h + 1) * hd].astype(BF16)
        qlatT_ref[h] = _dot_nt(wukt_ref[h], qh).astype(BF16)
    ckv = _dot(xb, wckv_ref[...])
    ckv_ref[...] = ckv.astype(BF16)
    ckvT_ref[...] = ckv.T.astype(BF16)

    qi3 = _dot_nt(wqi3t_ref[...], xb)
    hi, lo = _split_hi_lo(qi3)
    row = lax.broadcasted_iota(I32, qi3.shape, 0) % (3 * IDX_DIM)
    is_lo = (row >= IDX_DIM) & (row < 2 * IDX_DIM)
    qsel = jnp.where(is_lo, lo, hi)
    for h in range(IDX_HEADS):
        qiT_ref[h] = qsel[h * 3 * IDX_DIM:(h + 1) * 3 * IDX_DIM, :]
    k3 = _dot(xb, wk3_ref[...])
    khi, klo = _split_hi_lo(k3)
    col = lax.broadcasted_iota(I32, k3.shape, 1)
    kcat_ref[...] = jnp.where(col >= 2 * IDX_DIM, klo, khi)
    wT_ref[...] = _dot_nt(wwt_ref[...], xb)


def _dsa_proj(x3, wq, wukt, wckv, wqi3t, wk3, wwt):
    bsz, t, d = x3.shape
    tt = DSA_TILE
    nt = t // tt
    lat = wckv.shape[1]
    outs = pl.pallas_call(
        _dsa_proj_kernel,
        grid=(bsz, nt),
        in_specs=[
            pl.BlockSpec((None, tt, d), lambda i, j: (i, j, 0)),
            _const_spec(wq.shape),
            _const_spec(wukt.shape),
            _const_spec(wckv.shape),
            _const_spec(wqi3t.shape),
            _const_spec(wk3.shape),
            _const_spec(wwt.shape),
        ],
        out_specs=[
            pl.BlockSpec((None, N_HEADS, lat, tt), lambda i, j: (i, 0, 0, j)),
            pl.BlockSpec((None, tt, lat), lambda i, j: (i, j, 0)),
            pl.BlockSpec((None, None, lat, tt), lambda i, j: (i, j, 0, 0)),
            pl.BlockSpec((None, IDX_HEADS, 3 * IDX_DIM, tt), lambda i, j: (i, 0, 0, j)),
            pl.BlockSpec((None, tt, 3 * IDX_DIM), lambda i, j: (i, j, 0)),
            pl.BlockSpec((None, IDX_HEADS, tt), lambda i, j: (i, 0, j)),
        ],
        out_shape=[
            jax.ShapeDtypeStruct((bsz, N_HEADS, lat, t), BF16),
            jax.ShapeDtypeStruct((bsz, t, lat), BF16),
            jax.ShapeDtypeStruct((bsz, nt, lat, tt), BF16),
            jax.ShapeDtypeStruct((bsz, IDX_HEADS, 3 * IDX_DIM, t), BF16),
            jax.ShapeDtypeStruct((bsz, t, 3 * IDX_DIM), BF16),
            jax.ShapeDtypeStruct((bsz, IDX_HEADS, t), F32),
        ],
        compiler_params=_cparams(2),
        name="dsa_proj",
    )(x3, wq, wukt, wckv, wqi3t, wk3, wwt)
    return outs


def _dsa_attn_kernel(x_ref, qiT_ref, wT_ref, qlatT_ref, kcat_ref, ckv_ref, ckvT_ref,
                     kcat_m_ref, ckv_m_ref, ckvT_m_ref, wuvt_ref, wout_ref, g_ref, b_ref,
                     o_ref, keys, bias, outT, *, meta_only, topk, head_dim):
    tt = DSA_TILE
    j = pl.program_id(1)
    nx = 0 if meta_only else j + 1
    row = lax.broadcasted_iota(I32, (tt, tt), 0)
    col = lax.broadcasted_iota(I32, (tt, tt), 1)

    def score_keys(kc, allowed):
        acc = jnp.zeros((tt, tt), F32)
        for h in range(IDX_HEADS):
            s = _dot(kc, qiT_ref[h])
            acc = acc + jnp.maximum(s, 0.0) * wT_ref[h:h + 1, :]
        acc = jnp.where(acc == 0.0, 0.0, acc)
        acc = jnp.where(allowed, acc, NEG_INF)
        bits = pltpu.bitcast(acc, I32)
        return jnp.where(bits < 0, bits ^ 0x7FFFFFFF, bits)

    keys[0] = score_keys(kcat_m_ref[...], row < N_META)

    def score_body(t, carry):
        kc = kcat_ref[pl.ds(pl.multiple_of(t * tt, tt), tt), :]
        kchunk = t * (tt // CHUNK) + row // CHUNK
        qchunk = j * (tt // CHUNK) + col // CHUNK
        keys[1 + t] = score_keys(kc, kchunk <= qchunk)
        return carry

    lax.fori_loop(0, nx, score_body, 0)

    def count_ge(cand):
        def tile_count(idx, acc8):
            hit = jnp.where(keys[idx] >= cand, 1, 0)
            return acc8 + hit.reshape(tt // 8, 8, tt).sum(axis=0)
        acc8 = tile_count(0, jnp.zeros((8, tt), I32))
        acc8 = lax.fori_loop(0, nx, lambda t, a: tile_count(1 + t, a), acc8)
        return acc8.sum(axis=0, keepdims=True)

    def bisect(i, thr):
        cand = thr + lax.shift_left(jnp.int32(1), 31 - i)
        return jnp.where(count_ge(cand) >= topk, cand, thr)

    thr = lax.fori_loop(0, 32, bisect, jnp.full((1, tt), INT_MIN, I32))
    need = (topk - count_ge(thr + 1)).astype(F32)

    ltri = jnp.where(col < row, 1.0, 0.0).astype(BF16)

    def bias_tile(idx, taken):
        k = keys[idx]
        eq = k == thr
        eqf = jnp.where(eq, 1.0, 0.0)
        before = _dot(ltri, eqf.astype(BF16)) + taken
        sel = ((k > thr) | (eq & (before < need))) & (k > KEY_NEG_INF)
        bias[idx] = jnp.where(sel, 0.0, NEG_INF)
        return taken + eqf.sum(axis=0, keepdims=True)

    taken = bias_tile(0, jnp.zeros((1, tt), F32))
    lax.fori_loop(0, nx, lambda t, tk: bias_tile(1 + t, tk), taken)

    c_exp = (head_dim ** -0.5) * 1.4426950408889634

    def head_body(h, carry):
        ql = qlatT_ref[h]

        lg = _dot(ckv_m_ref[...], ql) + bias[0]
        m = jnp.maximum(lg.max(axis=0, keepdims=True), MAX_FLOOR)
        p = jnp.exp2((lg - m) * c_exp)
        l = p.sum(axis=0, keepdims=True)
        acc = _dot(ckvT_m_ref[...], p.astype(BF16))

        def tile_step(t, mla):
            m, l, acc = mla
            kv = ckv_ref[pl.ds(pl.multiple_of(t * tt, tt), tt), :]
            lg = _dot(kv, ql) + bias[1 + t]
            m_new = jnp.maximum(m, lg.max(axis=0, keepdims=True))
            a = jnp.exp2((m - m_new) * c_exp)
            p = jnp.exp2((lg - m_new) * c_exp)
            l = a * l + p.sum(axis=0, keepdims=True)
            acc = a * acc + _dot(ckvT_ref[t], p.astype(BF16))
            return m_new, l, acc

        m, l, acc = lax.fori_loop(0, nx, tile_step, (m, l, acc))
        ctx = (acc / l).astype(BF16)
        hd = wuvt_ref.shape[1]
        outT[pl.ds(pl.multiple_of(h * hd, hd), hd), :] = _dot(wuvt_ref[h], ctx)
        return carry

    lax.fori_loop(0, N_HEADS, head_body, 0)

    out = outT[...].T.astype(BF16)
    mix = _dot(out, wout_ref[...])
    o_ref[...] = _layer_norm(ALPHA * x_ref[...] + mix, g_ref[...], b_ref[...])


def _dsa_attn(x3, proj, proj_meta, wuvt, w_out, g, b, *, meta_only, topk):
    bsz, t, d = x3.shape
    tt = DSA_TILE
    nt = t // tt
    qlatT, ckv, ckvT, qiT, kcat, wT = proj
    _, ckv_m, ckvT_m, _, kcat_m, _ = proj_meta
    lat = ckv.shape[-1]
    kcat_m, ckv_m, ckvT_m = kcat_m[0], ckv_m[0], ckvT_m[0, 0]
    head_dim = wuvt.shape[1]
    return pl.pallas_call(
        functools.partial(_dsa_attn_kernel, meta_only=meta_only, topk=topk, head_dim=head_dim),
        grid=(bsz, nt),
        in_specs=[
            pl.BlockSpec((None, tt, d), lambda i, j: (i, j, 0)),
            pl.BlockSpec((None, IDX_HEADS, 3 * IDX_DIM, tt), lambda i, j: (i, 0, 0, j)),
            pl.BlockSpec((None, IDX_HEADS, tt), lambda i, j: (i, 0, j)),
            pl.BlockSpec((None, N_HEADS, lat, tt), lambda i, j: (i, 0, 0, j)),
            pl.BlockSpec((None, t, 3 * IDX_DIM), lambda i, j: (i, 0, 0)),
            pl.BlockSpec((None, t, lat), lambda i, j: (i, 0, 0)),
            pl.BlockSpec((None, nt, lat, tt), lambda i, j: (i, 0, 0, 0)),
            _const_spec(kcat_m.shape),
            _const_spec(ckv_m.shape),
            _const_spec(ckvT_m.shape),
            _const_spec(wuvt.shape),
            _const_spec(w_out.shape),
            _const_spec(g.shape),
            _const_spec(b.shape),
        ],
        out_specs=pl.BlockSpec((None, tt, d), lambda i, j: (i, j, 0)),
        out_shape=jax.ShapeDtypeStruct((bsz, t, d), F32),
        scratch_shapes=[
            pltpu.VMEM((nt + 1, tt, tt), I32),
            pltpu.VMEM((nt + 1, tt, tt), F32),
            pltpu.VMEM((d, tt), F32),
        ],
        compiler_params=_cparams(2),
        name="dsa_attn_ln",
    )(x3, qiT, wT, qlatT, kcat, ckv, ckvT, kcat_m, ckv_m, ckvT_m, wuvt, w_out, g, b)


def _row(v):
    return v.reshape(1, -1)


def kernel(x, meta, a_w_in, a_w_group, a_scale, a_w_out, b_w_in, b_w_uk, b_w_uv, b_w_out,
           c_w_in, c_conv, c_w_out, ln_mix_g, ln_mix_b, ffn_w_gu, ffn_w_down, ln_ffn_g, ln_ffn_b):
    bsz, seq, d = x.shape
    assert seq % DSA_TILE == 0 and meta.shape[0] == N_META
    topk = min(TOPK_MAX, seq // 4)
    lat = d // KV_LATENT_DIV
    row_tile = 512 if seq % 512 == 0 else DSA_TILE

    hx = x
    hm = meta.astype(x.dtype)

    for i in range(DEPTH):
        kind, jj = i % 3, i // 3
        g, b = _row(ln_mix_g[i]), _row(ln_mix_b[i])
        if kind == 0:
            args = (a_w_in[jj].astype(BF16), a_w_group[jj].astype(BF16), _row(a_scale[jj]),
                    a_w_out[jj].astype(BF16), g, b)
            hx_new = _pool_layer(hx, hm, *args, tq=row_tile, is_meta=False)
            hm = _pool_layer(hm[None], hm, *args, tq=N_META, is_meta=True)[0]
            hx = hx_new
        elif kind == 1:
            w = b_w_in[jj]
            o = 0
            wq = w[:, o:o + d].astype(BF16); o += d
            wckv = w[:, o:o + lat].astype(BF16); o += lat
            wqi = w[:, o:o + IDX_HEADS * IDX_DIM]; o += IDX_HEADS * IDX_DIM
            wk = w[:, o:o + IDX_DIM]; o += IDX_DIM
            ww = w[:, o:o + IDX_HEADS]
            wqi3t = jnp.tile(wqi.T.reshape(IDX_HEADS, 1, IDX_DIM, d), (1, 3, 1, 1))
            wqi3t = wqi3t.reshape(IDX_HEADS * 3 * IDX_DIM, d).astype(BF16)
            wk3 = jnp.tile(wk, (1, 3)).astype(BF16)
            wwt = ww.T.astype(BF16)
            wukt = jnp.swapaxes(b_w_uk[jj], 1, 2).astype(BF16)
            wuvt = jnp.swapaxes(b_w_uv[jj], 1, 2).astype(BF16)
            wout = b_w_out[jj].astype(BF16)
            pw = (wq, wukt, wckv, wqi3t, wk3, wwt)
            hm_pad = jnp.pad(hm, ((0, DSA_TILE - N_META), (0, 0)))[None]
            proj_m = _dsa_proj(hm_pad, *pw)
            proj_x = _dsa_proj(hx, *pw)
            hx_new = _dsa_attn(hx, proj_x, proj_m, wuvt, wout, g, b, meta_only=False, topk=topk)
            hm = _dsa_attn(hm_pad, proj_m, proj_m, wuvt, wout, g, b, meta_only=True,
                           topk=topk)[0, :N_META]
            hx = hx_new
        else:
            args = (c_w_in[jj].astype(BF16), c_conv[jj], c_w_out[jj].astype(BF16), g, b)
            hx_new = _conv_layer(hx, hm, *args, tq=row_tile, is_meta=False)
            hm = _conv_layer(hm[None], hm, *args, tq=N_META, is_meta=True)[0]
            hx = hx_new

        fargs = (ffn_w_gu[i].astype(BF16), ffn_w_down[i].astype(BF16),
                 _row(ln_ffn_g[i]), _row(ln_ffn_b[i]))
        hx = _ffn(hx.reshape(bsz * seq, d), *fargs, tile=row_tile).reshape(bsz, seq, d)
        if i + 1 < DEPTH:
            hm = _ffn(hm, *fargs, tile=N_META)
    return hx
```

```python
import functools

import jax
import jax.numpy as jnp
from jax import lax
from jax.experimental import pallas as pl
from jax.experimental.pallas import tpu as pltpu

CHUNK = 64
N_META = 16
POOL_WINDOWS = (2, 4, 8, 16)
N_HEADS = 8
KV_LATENT_DIV = 4
IDX_HEADS = 8
IDX_DIM = 64
TOPK_MAX = 256
CONV_WIDTH = 3
DEPTH = 4
ALPHA = (2.0 * DEPTH) ** 0.25
LN_EPS = 1e-5

BF16 = jnp.bfloat16
F32 = jnp.float32
I32 = jnp.int32
I16 = jnp.int16

VMEM_LIMIT_BYTES = 56 * 1024 * 1024
DSA_TILE = 512
I16_ROWS = 16
I16_MIN = -(2 ** 15)
COUNT_CHAINS = 4
KEY_NEG_INF = -2139095041
NEG_INF = float("-inf")
MAX_FLOOR = -1e30


def _cparams(n_axes):
    return pltpu.CompilerParams(
        dimension_semantics=("arbitrary",) * n_axes,
        vmem_limit_bytes=VMEM_LIMIT_BYTES,
    )


def _const_spec(shape):
    nd = len(shape)
    return pl.BlockSpec(shape, lambda *_: (0,) * nd, pipeline_mode=pl.Buffered(1))


def _layer_norm(z, g, b):
    mu = jnp.mean(z, axis=-1, keepdims=True)
    zc = z - mu
    var = jnp.mean(zc * zc, axis=-1, keepdims=True)
    return zc * lax.rsqrt(var + LN_EPS) * g + b


def _dot(a, b):
    return jnp.dot(a, b, preferred_element_type=F32)


def _dot_nt(a, b):
    return lax.dot_general(a, b, (((1,), (1,)), ((), ())), preferred_element_type=F32)


def _ffn_kernel(h_ref, wgu_ref, wd_ref, g_ref, b_ref, o_ref, *, d_ff, ff_chunk):
    h = h_ref[...]
    hb = h.astype(BF16)
    acc = ALPHA * h
    for c in range(d_ff // ff_chunk):
        lo = c * ff_chunk
        gate = _dot(hb, wgu_ref[:, lo:lo + ff_chunk])
        up = _dot(hb, wgu_ref[:, d_ff + lo:d_ff + lo + ff_chunk])
        act = (gate * jax.nn.sigmoid(gate) * up).astype(BF16)
        acc = acc + _dot(act, wd_ref[lo:lo + ff_chunk, :])
    o_ref[...] = _layer_norm(acc, g_ref[...], b_ref[...])


def _ffn(h2d, w_gu, w_down, g, b, *, tile):
    n, d = h2d.shape
    d_ff = w_down.shape[0]
    ff_chunk = 256 if d_ff % 256 == 0 else d_ff
    return pl.pallas_call(
        functools.partial(_ffn_kernel, d_ff=d_ff, ff_chunk=ff_chunk),
        grid=(n // tile,),
        in_specs=[
            pl.BlockSpec((tile, d), lambda i: (i, 0)),
            _const_spec(w_gu.shape),
            _const_spec(w_down.shape),
            _const_spec(g.shape),
            _const_spec(b.shape),
        ],
        out_specs=pl.BlockSpec((tile, d), lambda i: (i, 0)),
        out_shape=jax.ShapeDtypeStruct((n, d), F32),
        compiler_params=_cparams(1),
        name="ffn_ln",
    )(h2d, w_gu, w_down, g, b)


POOL_HALO = 16


def _pool_kernel(x_ref, meta_ref, win_ref, wg_ref, sc_ref, wout_ref, g_ref, b_ref,
                 o_ref, ubuf, *, tq, is_meta):
    j = pl.program_id(1)
    d = x_ref.shape[-1]
    gw = d // len(POOL_WINDOWS)
    x = x_ref[...]
    u = _dot(x.astype(BF16), win_ref[...])

    if is_meta:
        ubuf[0:POOL_HALO, :] = jnp.zeros((POOL_HALO, d), F32)
    else:
        @pl.when(j == 0)
        def _():
            ubuf[0:POOL_HALO, :] = _dot(meta_ref[...].astype(BF16), win_ref[...])

        @pl.when(j > 0)
        def _():
            ubuf[0:POOL_HALO, :] = ubuf[tq:tq + POOL_HALO, :]
    ubuf[POOL_HALO:POOL_HALO + tq, :] = u

    ys = []
    for gi, w in enumerate(POOL_WINDOWS):
        c0 = gi * gw
        s = u[:, c0:c0 + gw]
        for k in range(1, w):
            s = s + ubuf[POOL_HALO - k:POOL_HALO - k + tq, c0:c0 + gw]
        if is_meta:
            pos = lax.broadcasted_iota(I32, (tq, 1), 0) + 1
            cnt = jnp.minimum(pos, w).astype(F32)
            pooled = s / cnt
        else:
            pooled = s / float(w)
        dlt = (pooled - u[:, c0:c0 + gw]).astype(BF16)
        ys.append(_dot(dlt, wg_ref[gi]))
    y = jnp.concatenate(ys, axis=-1) * sc_ref[...]
    m = _dot(y.astype(BF16), wout_ref[...])
    o_ref[...] = _layer_norm(ALPHA * x + m, g_ref[...], b_ref[...])


def _pool_layer(x3, meta, w_in, w_group, scale, w_out, g, b, *, tq, is_meta):
    bsz, t, d = x3.shape
    return pl.pallas_call(
        functools.partial(_pool_kernel, tq=tq, is_meta=is_meta),
        grid=(bsz, t // tq),
        in_specs=[
            pl.BlockSpec((None, tq, d), lambda i, j: (i, j, 0)),
            _const_spec(meta.shape),
            _const_spec(w_in.shape),
            _const_spec(w_group.shape),
            _const_spec(scale.shape),
            _const_spec(w_out.shape),
            _const_spec(g.shape),
            _const_spec(b.shape),
        ],
        out_specs=pl.BlockSpec((None, tq, d), lambda i, j: (i, j, 0)),
        out_shape=jax.ShapeDtypeStruct((bsz, t, d), F32),
        scratch_shapes=[pltpu.VMEM((tq + POOL_HALO, d), F32)],
        compiler_params=_cparams(2),
        name="pool_ln",
    )(x3, meta, w_in, w_group, scale, w_out, g, b)


CONV_HALO = 8


def _conv_kernel(x_ref, meta_ref, win_ref, cw_ref, wout_ref, g_ref, b_ref,
                 o_ref, zbuf, *, tq, is_meta):
    j = pl.program_id(1)
    d = x_ref.shape[-1]
    x = x_ref[...]
    xb = x.astype(BF16)

    def gates(vb):
        bg = _dot(vb, win_ref[:, 0:d])
        cg = _dot(vb, win_ref[:, d:2 * d])
        hv = _dot(vb, win_ref[:, 2 * d:3 * d])
        return bg, cg * hv

    bg, z = gates(xb)
    if is_meta:
        zbuf[0:CONV_HALO, :] = jnp.zeros((CONV_HALO, d), F32)
    else:
        @pl.when(j == 0)
        def _():
            _, zm = gates(meta_ref[...].astype(BF16))
            zbuf[0:CONV_HALO, :] = zm[N_META - CONV_HALO:N_META, :]

        @pl.when(j > 0)
        def _():
            zbuf[0:CONV_HALO, :] = zbuf[tq:tq + CONV_HALO, :]
    zbuf[CONV_HALO:CONV_HALO + tq, :] = z

    conv = z * cw_ref[CONV_WIDTH - 1:CONV_WIDTH, :]
    for k in range(CONV_WIDTH - 1):
        sh = CONV_WIDTH - 1 - k
        conv = conv + zbuf[CONV_HALO - sh:CONV_HALO - sh + tq, :] * cw_ref[k:k + 1, :]
    m = _dot((bg * conv).astype(BF16), wout_ref[...])
    o_ref[...] = _layer_norm(ALPHA * x + m, g_ref[...], b_ref[...])


def _conv_layer(x3, meta, w_in, conv_w, w_out, g, b, *, tq, is_meta):
    bsz, t, d = x3.shape
    return pl.pallas_call(
        functools.partial(_conv_kernel, tq=tq, is_meta=is_meta),
        grid=(bsz, t // tq),
        in_specs=[
            pl.BlockSpec((None, tq, d), lambda i, j: (i, j, 0)),
            _const_spec(meta.shape),
            _const_spec(w_in.shape),
            _const_spec(conv_w.shape),
            _const_spec(w_out.shape),
            _const_spec(g.shape),
            _const_spec(b.shape),
        ],
        out_specs=pl.BlockSpec((None, tq, d), lambda i, j: (i, j, 0)),
        out_shape=jax.ShapeDtypeStruct((bsz, t, d), F32),
        scratch_shapes=[pltpu.VMEM((tq + CONV_HALO, d), F32)],
        compiler_params=_cparams(2),
        name="conv_ln",
    )(x3, meta, w_in, conv_w, w_out, g, b)


def _split_hi_lo(v):
    hi = v.astype(BF16)
    lo = (v - hi.astype(F32)).astype(BF16)
    return hi, lo


def _dsa_proj_kernel(x_ref, wq_ref, wukt_ref, wckv_ref, wqi3t_ref, wk3_ref, wwt_ref,
                     qlatT_ref, ckv_ref, ckvT_ref, qiT_ref, kcat_ref, wT_ref):
    xb = x_ref[...].astype(BF16)
    hd = wukt_ref.shape[2]
    q = _dot(xb, wq_ref[...])
    for h in range(N_HEADS):
        qh = q[:, h * hd:(h + 1) * hd].astype(BF16)
        qlatT_ref[h] = _dot_nt(wukt_ref[h], qh).astype(BF16)
    ckv = _dot(xb, wckv_ref[...])
    ckv_ref[...] = ckv.astype(BF16)
    ckvT_ref[...] = ckv.T.astype(BF16)

    qi3 = _dot_nt(wqi3t_ref[...], xb)
    hi, lo = _split_hi_lo(qi3)
    row = lax.broadcasted_iota(I32, qi3.shape, 0) % (3 * IDX_DIM)
    is_lo = (row >= IDX_DIM) & (row < 2 * IDX_DIM)
    qsel = jnp.where(is_lo, lo, hi)
    for h in range(IDX_HEADS):
        qiT_ref[h] = qsel[h * 3 * IDX_DIM:(h + 1) * 3 * IDX_DIM, :]
    k3 = _dot(xb, wk3_ref[...])
    khi, klo = _split_hi_lo(k3)
    col = lax.broadcasted_iota(I32, k3.shape, 1)
    kcat_ref[...] = jnp.where(col >= 2 * IDX_DIM, klo, khi)
    wT_ref[...] = _dot_nt(wwt_ref[...], xb)


def _dsa_proj(x3, wq, wukt, wckv, wqi3t, wk3, wwt):
    bsz, t, d = x3.shape
    tt = DSA_TILE
    nt = t // tt
    lat = wckv.shape[1]
    outs = pl.pallas_call(
        _dsa_proj_kernel,
        grid=(bsz, nt),
        in_specs=[
            pl.BlockSpec((None, tt, d), lambda i, j: (i, j, 0)),
            _const_spec(wq.shape),
            _const_spec(wukt.shape),
            _const_spec(wckv.shape),
            _const_spec(wqi3t.shape),
            _const_spec(wk3.shape),
            _const_spec(wwt.shape),
        ],
        out_specs=[
            pl.BlockSpec((None, N_HEADS, lat, tt), lambda i, j: (i, 0, 0, j)),
            pl.BlockSpec((None, tt, lat), lambda i, j: (i, j, 0)),
            pl.BlockSpec((None, None, lat, tt), lambda i, j: (i, j, 0, 0)),
            pl.BlockSpec((None, IDX_HEADS, 3 * IDX_DIM, tt), lambda i, j: (i, 0, 0, j)),
            pl.BlockSpec((None, tt, 3 * IDX_DIM), lambda i, j: (i, j, 0)),
            pl.BlockSpec((None, IDX_HEADS, tt), lambda i, j: (i, 0, j)),
        ],
        out_shape=[
            jax.ShapeDtypeStruct((bsz, N_HEADS, lat, t), BF16),
            jax.ShapeDtypeStruct((bsz, t, lat), BF16),
            jax.ShapeDtypeStruct((bsz, nt, lat, tt), BF16),
            jax.ShapeDtypeStruct((bsz, IDX_HEADS, 3 * IDX_DIM, t), BF16),
            jax.ShapeDtypeStruct((bsz, t, 3 * IDX_DIM), BF16),
            jax.ShapeDtypeStruct((bsz, IDX_HEADS, t), F32),
        ],
        compiler_params=_cparams(2),
        name="dsa_proj",
    )(x3, wq, wukt, wckv, wqi3t, wk3, wwt)
    return outs


def _dsa_attn_kernel(x_ref, qiT_ref, wT_ref, qlatT_ref, kcat_ref, ckv_ref, ckvT_ref,
                     kcat_m_ref, ckv_m_ref, ckvT_m_ref, wuvt_ref, wout_ref, g_ref, b_ref,
                     o_ref, keys, keys_m, k16, k16_m, acc_s, outT,
                     *, meta_only, topk, head_dim):
    tt = DSA_TILE
    j = pl.program_id(1)
    nx = 0 if meta_only else j + 1

    def tile_rows(t):
        return pl.ds(pl.multiple_of(t * tt, tt), tt)

    def score_keys(kc, allowed):
        acc = jnp.zeros((kc.shape[0], tt), F32)
        for h in range(IDX_HEADS):
            s = _dot(kc, qiT_ref[h])
            acc = acc + jnp.maximum(s, 0.0) * wT_ref[h:h + 1, :]
        acc = jnp.where(acc == 0.0, 0.0, acc)
        if allowed is not None:
            acc = jnp.where(allowed, acc, NEG_INF)
        bits = pltpu.bitcast(acc, I32)
        return jnp.where(bits < 0, bits ^ 0x7FFFFFFF, bits)

    def high_digit(k):
        return (k >> 16).astype(I16)

    km = score_keys(kcat_m_ref[...], None)
    keys_m[...] = km
    k16_m[...] = high_digit(km)

    def score_body(t, carry):
        row = lax.broadcasted_iota(I32, (tt, tt), 0)
        col = lax.broadcasted_iota(I32, (tt, tt), 1)
        kchunk = t * (tt // CHUNK) + row // CHUNK
        qchunk = j * (tt // CHUNK) + col // CHUNK
        k = score_keys(kcat_ref[tile_rows(t), :], kchunk <= qchunk)
        keys[t] = k
        k16[t] = high_digit(k)
        return carry

    lax.fori_loop(0, nx, score_body, 0)

    def count16(cand, strict):
        c16 = cand.astype(I16)

        def hits(v):
            return jnp.where((v > c16) if strict else (v >= c16), jnp.int16(1), jnp.int16(0))

        def body(t, accs):
            accs = list(accs)
            for i in range(tt // I16_ROWS):
                part = hits(k16[t, i * I16_ROWS:(i + 1) * I16_ROWS, :])
                accs[i % COUNT_CHAINS] = accs[i % COUNT_CHAINS] + part
            return tuple(accs)

        zero = jnp.zeros((I16_ROWS, tt), I16)
        accs = lax.fori_loop(0, nx, body, (hits(k16_m[...]),) + (zero,) * (COUNT_CHAINS - 1))
        total = accs[0].astype(I32)
        for a in accs[1:]:
            total = total + a.astype(I32)
        return total.sum(axis=0, keepdims=True)

    def bisect16(rank):
        def step(i, thr):
            cand = thr + lax.shift_left(jnp.int32(1), 15 - i)
            return jnp.where(count16(cand, False) >= rank, cand, thr)
        return lax.fori_loop(0, 16, step, jnp.full((1, tt), I16_MIN, I32))

    t_hi = bisect16(topk)
    c_above = count16(t_hi, True)
    c_ge_hi = count16(t_hi, False)

    def low_digit(k):
        lo = (k & 0xFFFF) + I16_MIN
        return jnp.where((k >> 16) == t_hi, lo, I16_MIN).astype(I16)

    k16_m[...] = low_digit(keys_m[...])

    def low_body(t, carry):
        k16[t] = low_digit(keys[t])
        return carry

    lax.fori_loop(0, nx, low_body, 0)

    t_lo = bisect16(topk - c_above)
    thr = t_hi * 65536 + (t_lo - I16_MIN)
    c_ge = jnp.where(t_lo > I16_MIN, c_above + count16(t_lo, False), c_ge_hi)
    excess = jnp.where(thr > KEY_NEG_INF, c_ge - topk, 0)
    has_ties = jnp.max(excess) > 0

    def as_bits(v):
        return pltpu.bitcast(v, I32)

    @pl.when(jnp.logical_not(has_ties))
    def _():
        thr_eff = jnp.maximum(thr, KEY_NEG_INF + 1)
        keys_m[...] = as_bits(jnp.where(keys_m[...] >= thr_eff, 0.0, NEG_INF))

        def body(t, carry):
            keys[t] = as_bits(jnp.where(keys[t] >= thr_eff, 0.0, NEG_INF))
            return carry

        lax.fori_loop(0, nx, body, 0)

    @pl.when(has_ties)
    def _():
        need = (topk - (c_above + count16(t_lo, True))).astype(F32)

        def select(k, before):
            eq = k == thr
            sel = ((k > thr) | (eq & (before < need))) & (k > KEY_NEG_INF)
            return jnp.where(sel, 0.0, NEG_INF), jnp.where(eq, 1.0, 0.0)

        def strictly_lower(n):
            r = lax.broadcasted_iota(I32, (n, n), 0)
            c = lax.broadcasted_iota(I32, (n, n), 1)
            return jnp.where(c < r, 1.0, 0.0).astype(BF16)

        eqm = jnp.where(keys_m[...] == thr, 1.0, 0.0)
        bm, _ = select(keys_m[...], _dot(strictly_lower(N_META), eqm.astype(BF16)))
        keys_m[...] = as_bits(bm)
        ltri = strictly_lower(tt)

        def body(t, taken):
            k = keys[t]
            eqf = jnp.where(k == thr, 1.0, 0.0)
            bt, _ = select(k, _dot(ltri, eqf.astype(BF16)) + taken)
            keys[t] = as_bits(bt)
            return taken + eqf.sum(axis=0, keepdims=True)

        lax.fori_loop(0, nx, body, eqm.sum(axis=0, keepdims=True))

    c_exp = (head_dim ** -0.5) * 1.4426950408889634

    bias_m = pltpu.bitcast(keys_m[...], F32)
    ms, ls = [], []
    for h in range(N_HEADS):
        lg = _dot(ckv_m_ref[...], qlatT_ref[h]) + bias_m
        m = jnp.maximum(lg.max(axis=0, keepdims=True), MAX_FLOOR)
        p = jnp.exp2((lg - m) * c_exp)
        ms.append(m)
        ls.append(p.sum(axis=0, keepdims=True))
        acc_s[h] = _dot(ckvT_m_ref[...], p.astype(BF16))

    def tile_step(t, carry):
        m_all, l_all = carry
        kv = ckv_ref[tile_rows(t), :]
        kvT = ckvT_ref[t]
        bt = pltpu.bitcast(keys[t], F32)
        ms, ls = [], []
        for h in range(N_HEADS):
            lg = _dot(kv, qlatT_ref[h]) + bt
            m_old = m_all[h:h + 1, :]
            m_new = jnp.maximum(m_old, lg.max(axis=0, keepdims=True))
            a = jnp.exp2((m_old - m_new) * c_exp)
            p = jnp.exp2((lg - m_new) * c_exp)
            ms.append(m_new)
            ls.append(a * l_all[h:h + 1, :] + p.sum(axis=0, keepdims=True))
            acc_s[h] = a * acc_s[h] + _dot(kvT, p.astype(BF16))
        return jnp.concatenate(ms, axis=0), jnp.concatenate(ls, axis=0)

    _, l_all = lax.fori_loop(0, nx, tile_step,
                             (jnp.concatenate(ms, axis=0), jnp.concatenate(ls, axis=0)))

    hd = wuvt_ref.shape[1]
    inv_l = 1.0 / l_all
    for h in range(N_HEADS):
        ctx = (acc_s[h] * inv_l[h:h + 1, :]).astype(BF16)
        outT[h * hd:(h + 1) * hd, :] = _dot(wuvt_ref[h], ctx)

    out = outT[...].T.astype(BF16)
    mix = _dot(out, wout_ref[...])
    o_ref[...] = _layer_norm(ALPHA * x_ref[...] + mix, g_ref[...], b_ref[...])


def _dsa_attn(x3, proj, proj_meta, wuvt, w_out, g, b, *, meta_only, topk):
    bsz, t, d = x3.shape
    tt = DSA_TILE
    nt = t // tt
    qlatT, ckv, ckvT, qiT, kcat, wT = proj
    _, ckv_m, ckvT_m, _, kcat_m, _ = proj_meta
    lat = ckv.shape[-1]
    kcat_m, ckv_m, ckvT_m = kcat_m[0, :N_META], ckv_m[0, :N_META], ckvT_m[0, 0, :, :N_META]
    head_dim = wuvt.shape[1]
    return pl.pallas_call(
        functools.partial(_dsa_attn_kernel, meta_only=meta_only, topk=topk, head_dim=head_dim),
        grid=(bsz, nt),
        in_specs=[
            pl.BlockSpec((None, tt, d), lambda i, j: (i, j, 0)),
            pl.BlockSpec((None, IDX_HEADS, 3 * IDX_DIM, tt), lambda i, j: (i, 0, 0, j)),
            pl.BlockSpec((None, IDX_HEADS, tt), lambda i, j: (i, 0, j)),
            pl.BlockSpec((None, N_HEADS, lat, tt), lambda i, j: (i, 0, 0, j)),
            pl.BlockSpec((None, t, 3 * IDX_DIM), lambda i, j: (i, 0, 0),
                         pipeline_mode=pl.Buffered(1)),
            pl.BlockSpec((None, t, lat), lambda i, j: (i, 0, 0),
                         pipeline_mode=pl.Buffered(1)),
            pl.BlockSpec((None, nt, lat, tt), lambda i, j: (i, 0, 0, 0),
                         pipeline_mode=pl.Buffered(1)),
            _const_spec(kcat_m.shape),
            _const_spec(ckv_m.shape),
            _const_spec(ckvT_m.shape),
            _const_spec(wuvt.shape),
            _const_spec(w_out.shape),
            _const_spec(g.shape),
            _const_spec(b.shape),
        ],
        out_specs=pl.BlockSpec((None, tt, d), lambda i, j: (i, j, 0)),
        out_shape=jax.ShapeDtypeStruct((bsz, t, d), F32),
        scratch_shapes=[
            pltpu.VMEM((nt, tt, tt), I32),
            pltpu.VMEM((N_META, tt), I32),
            pltpu.VMEM((nt, tt, tt), I16),
            pltpu.VMEM((N_META, tt), I16),
            pltpu.VMEM((N_HEADS, lat, tt), F32),
            pltpu.VMEM((d, tt), F32),
        ],
        compiler_params=_cparams(2),
        name="dsa_attn_ln",
    )(x3, qiT, wT, qlatT, kcat, ckv, ckvT, kcat_m, ckv_m, ckvT_m, wuvt, w_out, g, b)


def _row(v):
    return v.reshape(1, -1)


def kernel(x, meta, a_w_in, a_w_group, a_scale, a_w_out, b_w_in, b_w_uk, b_w_uv, b_w_out,
           c_w_in, c_conv, c_w_out, ln_mix_g, ln_mix_b, ffn_w_gu, ffn_w_down, ln_ffn_g, ln_ffn_b):
    bsz, seq, d = x.shape
    assert seq % DSA_TILE == 0 and meta.shape[0] == N_META
    topk = min(TOPK_MAX, seq // 4)
    lat = d // KV_LATENT_DIV
    row_tile = 512 if seq % 512 == 0 else DSA_TILE

    hx = x
    hm = meta.astype(x.dtype)

    for i in range(DEPTH):
        kind, jj = i % 3, i // 3
        g, b = _row(ln_mix_g[i]), _row(ln_mix_b[i])
        if kind == 0:
            args = (a_w_in[jj].astype(BF16), a_w_group[jj].astype(BF16), _row(a_scale[jj]),
                    a_w_out[jj].astype(BF16), g, b)
            hx_new = _pool_layer(hx, hm, *args, tq=row_tile, is_meta=False)
            hm = _pool_layer(hm[None], hm, *args, tq=N_META, is_meta=True)[0]
            hx = hx_new
        elif kind == 1:
            w = b_w_in[jj]
            o = 0
            wq = w[:, o:o + d].astype(BF16); o += d
            wckv = w[:, o:o + lat].astype(BF16); o += lat
            wqi = w[:, o:o + IDX_HEADS * IDX_DIM]; o += IDX_HEADS * IDX_DIM
            wk = w[:, o:o + IDX_DIM]; o += IDX_DIM
            ww = w[:, o:o + IDX_HEADS]
            wqi3t = jnp.tile(wqi.T.reshape(IDX_HEADS, 1, IDX_DIM, d), (1, 3, 1, 1))
            wqi3t = wqi3t.reshape(IDX_HEADS * 3 * IDX_DIM, d).astype(BF16)
            wk3 = jnp.tile(wk, (1, 3)).astype(BF16)
            wwt = ww.T.astype(BF16)
            wukt = jnp.swapaxes(b_w_uk[jj], 1, 2).astype(BF16)
            wuvt = jnp.swapaxes(b_w_uv[jj], 1, 2).astype(BF16)
            wout = b_w_out[jj].astype(BF16)
            pw = (wq, wukt, wckv, wqi3t, wk3, wwt)
            hm_pad = jnp.pad(hm, ((0, DSA_TILE - N_META), (0, 0)))[None]
            proj_m = _dsa_proj(hm_pad, *pw)
            proj_x = _dsa_proj(hx, *pw)
            hx_new = _dsa_attn(hx, proj_x, proj_m, wuvt, wout, g, b, meta_only=False, topk=topk)
            hm = _dsa_attn(hm_pad, proj_m, proj_m, wuvt, wout, g, b, meta_only=True,
                           topk=topk)[0, :N_META]
            hx = hx_new
        else:
            args = (c_w_in[jj].astype(BF16), c_conv[jj], c_w_out[jj].astype(BF16), g, b)
            hx_new = _conv_layer(hx, hm, *args, tq=row_tile, is_meta=False)
            hm = _conv_layer(hm[None], hm, *args, tq=N_META, is_meta=True)[0]
            hx = hx_new

        fargs = (ffn_w_gu[i].astype(BF16), ffn_w_down[i].astype(BF16),
                 _row(ln_ffn_g[i]), _row(ln_ffn_b[i]))
        hx = _ffn(hx.reshape(bsz * seq, d), *fargs, tile=row_tile).reshape(bsz, seq, d)
        if i + 1 < DEPTH:
            hm = _ffn(hm, *fargs, tile=N_META)
    return hx
```

```python
import functools

import jax
import jax.numpy as jnp
from jax import lax
from jax.experimental import pallas as pl
from jax.experimental.pallas import tpu as pltpu

CHUNK = 64
N_META = 16
POOL_WINDOWS = (2, 4, 8, 16)
N_HEADS = 8
KV_LATENT_DIV = 4
IDX_HEADS = 8
IDX_DIM = 64
TOPK_MAX = 256
CONV_WIDTH = 3
DEPTH = 4
ALPHA = (2.0 * DEPTH) ** 0.25
LN_EPS = 1e-5

BF16 = jnp.bfloat16
F32 = jnp.float32
I32 = jnp.int32

VMEM_LIMIT_BYTES = 56 * 1024 * 1024
DSA_TILE = 512
I16_ROWS = 16
INT_MIN = -(2 ** 31)
FLT_MAX = 3.4028234663852886e38
FLT_TINY = 1.1754943508222875e-38
BRACKET_REL = 1e-6
MAX_SEARCH_ITERS = 24
COUNT_CHAINS = 4
SOFTMAX_CHAINS = 4
NEG_INF = float("-inf")
MAX_FLOOR = -1e30


def _cparams(n_axes):
    return pltpu.CompilerParams(
        dimension_semantics=("arbitrary",) * n_axes,
        vmem_limit_bytes=VMEM_LIMIT_BYTES,
    )


def _const_spec(shape):
    nd = len(shape)
    return pl.BlockSpec(shape, lambda *_: (0,) * nd, pipeline_mode=pl.Buffered(1))


def _layer_norm(z, g, b):
    mu = jnp.mean(z, axis=-1, keepdims=True)
    zc = z - mu
    var = jnp.mean(zc * zc, axis=-1, keepdims=True)
    return zc * lax.rsqrt(var + LN_EPS) * g + b


def _dot(a, b):
    return jnp.dot(a, b, preferred_element_type=F32)


def _dot_nt(a, b):
    return lax.dot_general(a, b, (((1,), (1,)), ((), ())), preferred_element_type=F32)


FFN_SUB_ROWS = 512

def _ffn_kernel(h_ref, wgu_ref, wd_ref, g_ref, b_ref, o_ref, *, d_ff, ff_chunk):
    rows = h_ref.shape[0]
    sub = min(rows, FFN_SUB_ROWS)
    for r0 in range(0, rows, sub):
        h = h_ref[r0:r0 + sub, :]
        hb = h.astype(BF16)
        acc = ALPHA * h
        for c in range(d_ff // ff_chunk):
            lo = c * ff_chunk
            gate = _dot(hb, wgu_ref[:, lo:lo + ff_chunk])
            up = _dot(hb, wgu_ref[:, d_ff + lo:d_ff + lo + ff_chunk])
            act = (gate * jax.nn.sigmoid(gate) * up).astype(BF16)
            acc = acc + _dot(act, wd_ref[lo:lo + ff_chunk, :])
        o_ref[r0:r0 + sub, :] = _layer_norm(acc, g_ref[...], b_ref[...])


def _ffn(h2d, w_gu, w_down, g, b, *, tile):
    n, d = h2d.shape
    d_ff = w_down.shape[0]
    ff_chunk = 256 if d_ff % 256 == 0 else d_ff
    return pl.pallas_call(
        functools.partial(_ffn_kernel, d_ff=d_ff, ff_chunk=ff_chunk),
        grid=(n // tile,),
        in_specs=[
            pl.BlockSpec((tile, d), lambda i: (i, 0)),
            _const_spec(w_gu.shape),
            _const_spec(w_down.shape),
            _const_spec(g.shape),
            _const_spec(b.shape),
        ],
        out_specs=pl.BlockSpec((tile, d), lambda i: (i, 0)),
        out_shape=jax.ShapeDtypeStruct((n, d), F32),
        compiler_params=_cparams(1),
        name="ffn_ln",
    )(h2d, w_gu, w_down, g, b)


POOL_HALO = 16
POOL_PAD = 8
MIXER_SUB_ROWS = 512


def _pool_kernel(x_ref, meta_ref, win_ref, wg_ref, sc_ref, wout_ref, g_ref, b_ref,
                 o_ref, ubuf, s2, s4, s8, *, tq, is_meta):
    j = pl.program_id(1)
    d = x_ref.shape[-1]
    gw = d // len(POOL_WINDOWS)
    sub = min(tq, MIXER_SUB_ROWS)
    pad, top = POOL_PAD, POOL_PAD + POOL_HALO

    ubuf[0:pad, :] = jnp.zeros((pad, d), F32)
    if is_meta:
        ubuf[pad:top, :] = jnp.zeros((POOL_HALO, d), F32)
    else:
        @pl.when(j == 0)
        def _():
            ubuf[pad:top, :] = _dot(meta_ref[...].astype(BF16), win_ref[...])

        @pl.when(j > 0)
        def _():
            ubuf[pad:top, :] = ubuf[pad + tq:top + tq, :]

    for si, r0 in enumerate(range(0, tq, sub)):
        x = x_ref[r0:r0 + sub, :]
        u = _dot(x.astype(BF16), win_ref[...])
        base = top + r0
        ubuf[base:base + sub, :] = u

        ext = sub + POOL_HALO
        e0 = base - POOL_HALO
        s2[si, 0:pad, :] = jnp.zeros((pad, d), F32)
        s2[si, pad:pad + ext, :] = ubuf[e0:e0 + ext, :] + ubuf[e0 - 1:e0 - 1 + ext, :]
        s4[si, 0:pad, :] = jnp.zeros((pad, d - gw), F32)
        s4[si, pad:pad + ext, :] = s2[si, pad:pad + ext, gw:] + s2[si, pad - 2:pad - 2 + ext, gw:]
        s8[si, :, :] = s4[si, pad:pad + ext, gw:] + s4[si, pad - 4:pad - 4 + ext, gw:]
        t0 = pad + POOL_HALO
        win_sums = (
            s2[si, t0:t0 + sub, 0:gw],
            s4[si, t0:t0 + sub, 0:gw],
            s8[si, POOL_HALO:POOL_HALO + sub, 0:gw],
            s8[si, POOL_HALO:POOL_HALO + sub, gw:] + s8[si, POOL_HALO - 8:POOL_HALO - 8 + sub, gw:],
        )

        ys = []
        for gi, w in enumerate(POOL_WINDOWS):
            c0 = gi * gw
            s = win_sums[gi]
            if is_meta:
                pos = lax.broadcasted_iota(I32, (sub, 1), 0) + 1
                cnt = jnp.minimum(pos, w).astype(F32)
                pooled = s / cnt
            else:
                pooled = s / float(w)
            dlt = (pooled - u[:, c0:c0 + gw]).astype(BF16)
            ys.append(_dot(dlt, wg_ref[gi]))
        y = jnp.concatenate(ys, axis=-1) * sc_ref[...]
        m = _dot(y.astype(BF16), wout_ref[...])
        o_ref[r0:r0 + sub, :] = _layer_norm(ALPHA * x + m, g_ref[...], b_ref[...])


def _pool_layer(x3, meta, w_in, w_group, scale, w_out, g, b, *, tq, is_meta):
    bsz, t, d = x3.shape
    assert POOL_WINDOWS == (2, 4, 8, 16)
    gw = d // len(POOL_WINDOWS)
    sub = min(tq, MIXER_SUB_ROWS)
    n_sub, ext = tq // sub, sub + POOL_HALO
    return pl.pallas_call(
        functools.partial(_pool_kernel, tq=tq, is_meta=is_meta),
        grid=(bsz, t // tq),
        in_specs=[
            pl.BlockSpec((None, tq, d), lambda i, j: (i, j, 0)),
            _const_spec(meta.shape),
            _const_spec(w_in.shape),
            _const_spec(w_group.shape),
            _const_spec(scale.shape),
            _const_spec(w_out.shape),
            _const_spec(g.shape),
            _const_spec(b.shape),
        ],
        out_specs=pl.BlockSpec((None, tq, d), lambda i, j: (i, j, 0)),
        out_shape=jax.ShapeDtypeStruct((bsz, t, d), F32),
        scratch_shapes=[
            pltpu.VMEM((POOL_PAD + POOL_HALO + tq, d), F32),
            pltpu.VMEM((n_sub, POOL_PAD + ext, d), F32),
            pltpu.VMEM((n_sub, POOL_PAD + ext, d - gw), F32),
            pltpu.VMEM((n_sub, ext, d - 2 * gw), F32),
        ],
        compiler_params=_cparams(2),
        name="pool_ln",
    )(x3, meta, w_in, w_group, scale, w_out, g, b)


CONV_HALO = 8


def _conv_kernel(x_ref, meta_ref, win_ref, cw_ref, wout_ref, g_ref, b_ref,
                 o_ref, zbuf, *, tq, is_meta):
    j = pl.program_id(1)
    d = x_ref.shape[-1]
    sub = min(tq, MIXER_SUB_ROWS)

    def gates(vb):
        bg = _dot(vb, win_ref[:, 0:d])
        cg = _dot(vb, win_ref[:, d:2 * d])
        hv = _dot(vb, win_ref[:, 2 * d:3 * d])
        return bg, cg * hv

    if is_meta:
        zbuf[0:CONV_HALO, :] = jnp.zeros((CONV_HALO, d), F32)
    else:
        @pl.when(j == 0)
        def _():
            _, zm = gates(meta_ref[...].astype(BF16))
            zbuf[0:CONV_HALO, :] = zm[N_META - CONV_HALO:N_META, :]

        @pl.when(j > 0)
        def _():
            zbuf[0:CONV_HALO, :] = zbuf[tq:tq + CONV_HALO, :]

    for r0 in range(0, tq, sub):
        x = x_ref[r0:r0 + sub, :]
        bg, z = gates(x.astype(BF16))
        base = CONV_HALO + r0
        zbuf[base:base + sub, :] = z
        conv = z * cw_ref[CONV_WIDTH - 1:CONV_WIDTH, :]
        for k in range(CONV_WIDTH - 1):
            sh = CONV_WIDTH - 1 - k
            conv = conv + zbuf[base - sh:base - sh + sub, :] * cw_ref[k:k + 1, :]
        m = _dot((bg * conv).astype(BF16), wout_ref[...])
        o_ref[r0:r0 + sub, :] = _layer_norm(ALPHA * x + m, g_ref[...], b_ref[...])


def _conv_layer(x3, meta, w_in, conv_w, w_out, g, b, *, tq, is_meta):
    bsz, t, d = x3.shape
    return pl.pallas_call(
        functools.partial(_conv_kernel, tq=tq, is_meta=is_meta),
        grid=(bsz, t // tq),
        in_specs=[
            pl.BlockSpec((None, tq, d), lambda i, j: (i, j, 0)),
            _const_spec(meta.shape),
            _const_spec(w_in.shape),
            _const_spec(conv_w.shape),
            _const_spec(w_out.shape),
            _const_spec(g.shape),
            _const_spec(b.shape),
        ],
        out_specs=pl.BlockSpec((None, tq, d), lambda i, j: (i, j, 0)),
        out_shape=jax.ShapeDtypeStruct((bsz, t, d), F32),
        scratch_shapes=[pltpu.VMEM((tq + CONV_HALO, d), F32)],
        compiler_params=_cparams(2),
        name="conv_ln",
    )(x3, meta, w_in, conv_w, w_out, g, b)


def _split_hi_lo(v):
    hi = v.astype(BF16)
    lo = (v - hi.astype(F32)).astype(BF16)
    return hi, lo


def _dsa_proj_kernel(x_ref, wq_ref, wukt_ref, wckv_ref, wqi3t_ref, wk3_ref, wwt_ref,
                     qlatT_ref, ckv_ref, ckvT_ref, qiT_ref, kcat_ref, wT_ref):
    xb = x_ref[...].astype(BF16)
    hd = wukt_ref.shape[2]
    q = _dot(xb, wq_ref[...])
    for h in range(N_HEADS):
        qh = q[:, h * hd:(h + 1) * hd].astype(BF16)
        qlatT_ref[h] = _dot_nt(wukt_ref[h], qh).astype(BF16)
    ckv = _dot(xb, wckv_ref[...])
    ckv_ref[...] = ckv.astype(BF16)
    ckvT_ref[...] = ckv.T.astype(BF16)

    qi3 = _dot_nt(wqi3t_ref[...], xb)
    hi, lo = _split_hi_lo(qi3)
    row = lax.broadcasted_iota(I32, qi3.shape, 0) % (3 * IDX_DIM)
    is_lo = (row >= IDX_DIM) & (row < 2 * IDX_DIM)
    qsel = jnp.where(is_lo, lo, hi)
    for h in range(IDX_HEADS):
        qiT_ref[h] = qsel[h * 3 * IDX_DIM:(h + 1) * 3 * IDX_DIM, :]
    k3 = _dot(xb, wk3_ref[...])
    khi, klo = _split_hi_lo(k3)
    col = lax.broadcasted_iota(I32, k3.shape, 1)
    kcat_ref[...] = jnp.where(col >= 2 * IDX_DIM, klo, khi)
    wT_ref[...] = _dot_nt(wwt_ref[...], xb)


def _dsa_proj(x3, wq, wukt, wckv, wqi3t, wk3, wwt):
    bsz, t, d = x3.shape
    tt = DSA_TILE
    nt = t // tt
    lat = wckv.shape[1]
    outs = pl.pallas_call(
        _dsa_proj_kernel,
        grid=(bsz, nt),
        in_specs=[
            pl.BlockSpec((None, tt, d), lambda i, j: (i, j, 0)),
            _const_spec(wq.shape),
            _const_spec(wukt.shape),
            _const_spec(wckv.shape),
            _const_spec(wqi3t.shape),
            _const_spec(wk3.shape),
            _const_spec(wwt.shape),
        ],
        out_specs=[
            pl.BlockSpec((None, N_HEADS, lat, tt), lambda i, j: (i, 0, 0, j)),
            pl.BlockSpec((None, tt, lat), lambda i, j: (i, j, 0)),
            pl.BlockSpec((None, None, lat, tt), lambda i, j: (i, j, 0, 0)),
            pl.BlockSpec((None, IDX_HEADS, 3 * IDX_DIM, tt), lambda i, j: (i, 0, 0, j)),
            pl.BlockSpec((None, tt, 3 * IDX_DIM), lambda i, j: (i, j, 0)),
            pl.BlockSpec((None, IDX_HEADS, tt), lambda i, j: (i, 0, j)),
        ],
        out_shape=[
            jax.ShapeDtypeStruct((bsz, N_HEADS, lat, t), BF16),
            jax.ShapeDtypeStruct((bsz, t, lat), BF16),
            jax.ShapeDtypeStruct((bsz, nt, lat, tt), BF16),
            jax.ShapeDtypeStruct((bsz, IDX_HEADS, 3 * IDX_DIM, t), BF16),
            jax.ShapeDtypeStruct((bsz, t, 3 * IDX_DIM), BF16),
            jax.ShapeDtypeStruct((bsz, IDX_HEADS, t), F32),
        ],
        compiler_params=_cparams(2),
        name="dsa_proj",
    )(x3, wq, wukt, wckv, wqi3t, wk3, wwt)
    return outs


def _dsa_attn_kernel(x_ref, qiT_ref, wT_ref, qlatT_ref, kcat_ref, ckv_ref, ckvT_ref,
                     kcat_m_ref, ckv_m_ref, ckvT_m_ref, wuvt_ref, wout_ref, g_ref, b_ref,
                     o_ref, scores, scores_m, thr_s, acc_s, lg_a, lg_b, p_a, p_b, outT,
                     *, meta_only, topk, head_dim):
    tt = DSA_TILE
    j = pl.program_id(1)
    nx = 0 if meta_only else j + 1

    def tile_rows(t):
        return pl.ds(pl.multiple_of(t * tt, tt), tt)

    def raw_scores(kc):
        acc = jnp.zeros((kc.shape[0], tt), F32)
        for h in range(IDX_HEADS):
            s = _dot(kc, qiT_ref[h])
            acc = acc + jnp.maximum(s, 0.0) * wT_ref[h:h + 1, :]
        return acc

    sm = raw_scores(kcat_m_ref[...])
    scores_m[...] = sm

    def score_body(t, carry):
        mx, mn = carry
        row = lax.broadcasted_iota(I32, (tt, tt), 0)
        col = lax.broadcasted_iota(I32, (tt, tt), 1)
        kchunk = t * (tt // CHUNK) + row // CHUNK
        qchunk = j * (tt // CHUNK) + col // CHUNK
        acc = raw_scores(kcat_ref[tile_rows(t), :])
        scores[t] = jnp.where(kchunk <= qchunk, acc, NEG_INF)
        return (jnp.maximum(mx, acc.max(axis=0, keepdims=True)),
                jnp.minimum(mn, acc.min(axis=0, keepdims=True)))

    mx, mn = lax.fori_loop(0, nx, score_body, (sm.max(axis=0, keepdims=True),
                                               sm.min(axis=0, keepdims=True)))

    def chained_count(hits):
        def body(t, accs):
            accs = list(accs)
            for i in range(tt // 8):
                c = i % COUNT_CHAINS
                accs[c] = accs[c] + hits(scores[t, i * 8:(i + 1) * 8, :])
            return tuple(accs)

        zero = jnp.zeros((8, tt), hits(scores_m[0:8, :]).dtype)
        init = (hits(scores_m[0:8, :]), hits(scores_m[8:N_META, :])) + (zero,) * (COUNT_CHAINS - 2)
        accs = lax.fori_loop(0, nx, body, init)
        total = accs[0]
        for a in accs[1:]:
            total = total + a
        return total.sum(axis=0, keepdims=True)

    def count(thr, strict):
        return chained_count(lambda v: jnp.where((v > thr) if strict else (v >= thr), 1.0, 0.0))

    kf = float(topk)
    if meta_only:
        n_valid = jnp.full((1, tt), float(N_META), F32)
    else:
        qpos = j * tt + lax.broadcasted_iota(I32, (1, tt), 1)
        n_valid = (N_META + CHUNK * (qpos // CHUNK + 1)).astype(F32)
    all_sel = n_valid <= kf
    c0 = count(jnp.zeros((1, tt), F32), False)
    pos = c0 > kf
    one, zero = jnp.ones((1, tt), F32), jnp.zeros((1, tt), F32)
    done = jnp.where(all_sel | (c0 == kf), one, zero)
    tstar = jnp.where(all_sel, -FLT_MAX, zero)
    lo = jnp.where(pos, zero, mn - jnp.abs(mn) * BRACKET_REL - 2.0 * FLT_TINY)
    hi = jnp.where(pos, mx + jnp.abs(mx) * BRACKET_REL + 2.0 * FLT_TINY, zero)
    glo = jnp.where(pos, c0 - kf, n_valid - kf)
    ghi = jnp.where(pos, -kf, c0 - kf)

    def n_open(done, tie0, stuck):
        return jnp.sum(jnp.where((done + tie0 + stuck) == 0.0, 1.0, 0.0))

    def search_cond(c):
        return jnp.logical_and(c[0] < MAX_SEARCH_ITERS, c[1] > 0.0)

    def search_step(c):
        it, _, lo, hi, glo, ghi, done, tie0, stuck, last, tstar = c
        active = (done + tie0 + stuck) == 0.0
        cand = lo + (hi - lo) * (glo / (glo - ghi))
        cand = jnp.where((cand > lo) & (cand < hi), cand, 0.5 * lo + 0.5 * hi)
        zero_probe = jnp.logical_and(it == 0, lo == 0.0)
        cand = jnp.where(zero_probe, FLT_TINY, cand)
        inside = (cand > lo) & (cand < hi)
        g = count(cand, False) - kf
        ok = active & inside
        hit = ok & (g == 0.0)
        up = ok & (g > 0.0)
        dn = ok & (g < 0.0)
        tie_now = dn & zero_probe
        glo_n = jnp.where(up, g, jnp.where(dn & (last < 0.0), 0.5 * glo, glo))
        ghi_n = jnp.where(dn, g, jnp.where(up & (last > 0.0), 0.5 * ghi, ghi))
        done = jnp.where(hit, 1.0, done)
        tie0 = jnp.where(tie_now, 1.0, tie0)
        stuck = jnp.where(active & jnp.logical_not(inside), 1.0, stuck)
        return (it + 1, n_open(done, tie0, stuck), jnp.where(up, cand, lo), jnp.where(dn, cand, hi),
                glo_n, ghi_n, done, tie0, stuck,
                jnp.where(up, 1.0, jnp.where(dn, -1.0, last)), jnp.where(hit, cand, tstar))

    res = lax.while_loop(search_cond, search_step,
                         (jnp.int32(0), n_open(done, zero, zero), lo, hi, glo, ghi,
                          done, zero, zero, zero, tstar))
    done, tie0, tstar = res[6], res[7], res[10]
    resolved = (done + tie0) > 0.0
    thr_s[0:1, :] = jnp.where(tie0 > 0.0, 0.0, tstar)
    thr_s[1:2, :] = tie0

    @pl.when(jnp.sum(jnp.where(resolved, 0.0, 1.0)) > 0.0)
    def _():
        def int_key(v):
            bits = pltpu.bitcast(jnp.where(v == 0.0, 0.0, v), I32)
            return jnp.where(bits < 0, bits ^ 0x7FFFFFFF, bits)

        def bisect(i, kth):
            cand = kth + lax.shift_left(jnp.int32(1), 31 - i)
            cnt = chained_count(lambda v: jnp.where(int_key(v) >= cand, 1, 0))
            return jnp.where(cnt >= topk, cand, kth)

        kth = lax.fori_loop(0, 32, bisect, jnp.full((1, tt), INT_MIN, I32))
        t_fb = pltpu.bitcast(jnp.where(kth < 0, kth ^ 0x7FFFFFFF, kth), F32)
        ties_fb = jnp.where(count(t_fb, False) > kf, 1.0, 0.0)
        thr_s[0:1, :] = jnp.where(resolved, thr_s[0:1, :], t_fb)
        thr_s[1:2, :] = jnp.where(resolved, thr_s[1:2, :], ties_fb)

    thr = thr_s[0:1, :]
    has_ties = jnp.sum(thr_s[1:2, :]) > 0.0

    @pl.when(jnp.logical_not(has_ties))
    def _():
        scores_m[...] = jnp.where(scores_m[...] >= thr, 0.0, NEG_INF)

        def body(t, carry):
            scores[t] = jnp.where(scores[t] >= thr, 0.0, NEG_INF)
            return carry

        lax.fori_loop(0, nx, body, 0)

    @pl.when(has_ties)
    def _():
        need = kf - count(thr, True)

        def select(v, before):
            eq = v == thr
            sel = (v > thr) | (eq & (before < need))
            return jnp.where(sel, 0.0, NEG_INF)

        def strictly_lower(n):
            r = lax.broadcasted_iota(I32, (n, n), 0)
            c = lax.broadcasted_iota(I32, (n, n), 1)
            return jnp.where(c < r, 1.0, 0.0).astype(BF16)

        vm = scores_m[...]
        eqm = jnp.where(vm == thr, 1.0, 0.0)
        scores_m[...] = select(vm, _dot(strictly_lower(N_META), eqm.astype(BF16)))
        ltri = strictly_lower(tt)

        def body(t, taken):
            v = scores[t]
            eqf = jnp.where(v == thr, 1.0, 0.0)
            scores[t] = select(v, _dot(ltri, eqf.astype(BF16)) + taken)
            return taken + eqf.sum(axis=0, keepdims=True)

        lax.fori_loop(0, nx, body, eqm.sum(axis=0, keepdims=True))

    c_exp = (head_dim ** -0.5) * 1.4426950408889634

    bias_m = scores_m[...]
    ms, ls = [], []
    for h in range(N_HEADS):
        lg = _dot(ckv_m_ref[...], qlatT_ref[h]) + bias_m
        m = jnp.maximum(lg.max(axis=0, keepdims=True), MAX_FLOOR)
        p = jnp.exp2((lg - m) * c_exp)
        ms.append(m)
        ls.append(p.sum(axis=0, keepdims=True))
        acc_s[h] = _dot(ckvT_m_ref[...], p.astype(BF16))

    def tile_step(t, carry):
        m_all, l_all = carry
        kv = ckv_ref[tile_rows(t), :]
        kvT = ckvT_ref[t]
        ms, ls = [], []

        def tree(vals, op):
            while len(vals) > 1:
                vals = [op(a, b) for a, b in zip(vals[0::2], vals[1::2])]
            return vals[0]

        par = t & 1

        def biased_max(lg):
            accs = [None] * SOFTMAX_CHAINS
            for i in range(tt // 8):
                rows = slice(i * 8, (i + 1) * 8)
                x = lg[par, rows, :] + scores[t, rows, :]
                lg[par, rows, :] = x
                c = i % SOFTMAX_CHAINS
                accs[c] = x if accs[c] is None else jnp.maximum(accs[c], x)
            return tree(accs, jnp.maximum).max(axis=0, keepdims=True)

        def exp_chunks(lg, ps, m_new):
            accs = [None] * SOFTMAX_CHAINS
            for i in range(tt // I16_ROWS):
                rows = slice(i * I16_ROWS, (i + 1) * I16_ROWS)
                p = jnp.exp2((lg[par, rows, :] - m_new) * c_exp)
                ps[par, rows, :] = p.astype(BF16)
                c = i % SOFTMAX_CHAINS
                accs[c] = p if accs[c] is None else accs[c] + p
            return tree(accs, jnp.add).sum(axis=0, keepdims=True)

        lgs, pss = (lg_a, lg_b), (p_a, p_b)
        lgs[0][par] = _dot(kv, qlatT_ref[0])
        for h in range(N_HEADS):
            if h + 1 < N_HEADS:
                lgs[(h + 1) % 2][par] = _dot(kv, qlatT_ref[h + 1])
            lg, ps = lgs[h % 2], pss[h % 2]
            m_old = m_all[h:h + 1, :]
            m_new = jnp.maximum(m_old, biased_max(lg))
            a = jnp.exp2((m_old - m_new) * c_exp)
            l_tile = exp_chunks(lg, ps, m_new)
            ms.append(m_new)
            ls.append(a * l_all[h:h + 1, :] + l_tile)
            acc_s[h] = a * acc_s[h] + _dot(kvT, ps[par])
        return jnp.concatenate(ms, axis=0), jnp.concatenate(ls, axis=0)

    _, l_all = lax.fori_loop(0, nx, tile_step,
                             (jnp.concatenate(ms, axis=0), jnp.concatenate(ls, axis=0)))

    hd = wuvt_ref.shape[1]
    inv_l = 1.0 / l_all
    for h in range(N_HEADS):
        ctx = (acc_s[h] * inv_l[h:h + 1, :]).astype(BF16)
        outT[h * hd:(h + 1) * hd, :] = _dot(wuvt_ref[h], ctx)

    out = outT[...].T.astype(BF16)
    mix = _dot(out, wout_ref[...])
    o_ref[...] = _layer_norm(ALPHA * x_ref[...] + mix, g_ref[...], b_ref[...])


def _dsa_attn(x3, proj, proj_meta, wuvt, w_out, g, b, *, meta_only, topk):
    bsz, t, d = x3.shape
    tt = DSA_TILE
    nt = t // tt
    qlatT, ckv, ckvT, qiT, kcat, wT = proj
    _, ckv_m, ckvT_m, _, kcat_m, _ = proj_meta
    lat = ckv.shape[-1]
    kcat_m, ckv_m, ckvT_m = kcat_m[0, :N_META], ckv_m[0, :N_META], ckvT_m[0, 0, :, :N_META]
    head_dim = wuvt.shape[1]
    return pl.pallas_call(
        functools.partial(_dsa_attn_kernel, meta_only=meta_only, topk=topk, head_dim=head_dim),
        grid=(bsz, nt),
        in_specs=[
            pl.BlockSpec((None, tt, d), lambda i, j: (i, j, 0)),
            pl.BlockSpec((None, IDX_HEADS, 3 * IDX_DIM, tt), lambda i, j: (i, 0, 0, j)),
            pl.BlockSpec((None, IDX_HEADS, tt), lambda i, j: (i, 0, j)),
            pl.BlockSpec((None, N_HEADS, lat, tt), lambda i, j: (i, 0, 0, j)),
            pl.BlockSpec((None, t, 3 * IDX_DIM), lambda i, j: (i, 0, 0),
                         pipeline_mode=pl.Buffered(1)),
            pl.BlockSpec((None, t, lat), lambda i, j: (i, 0, 0),
                         pipeline_mode=pl.Buffered(1)),
            pl.BlockSpec((None, nt, lat, tt), lambda i, j: (i, 0, 0, 0),
                         pipeline_mode=pl.Buffered(1)),
            _const_spec(kcat_m.shape),
            _const_spec(ckv_m.shape),
            _const_spec(ckvT_m.shape),
            _const_spec(wuvt.shape),
            _const_spec(w_out.shape),
            _const_spec(g.shape),
            _const_spec(b.shape),
        ],
        out_specs=pl.BlockSpec((None, tt, d), lambda i, j: (i, j, 0)),
        out_shape=jax.ShapeDtypeStruct((bsz, t, d), F32),
        scratch_shapes=[
            pltpu.VMEM((nt, tt, tt), F32),
            pltpu.VMEM((N_META, tt), F32),
            pltpu.VMEM((8, tt), F32),
            pltpu.VMEM((N_HEADS, lat, tt), F32),
            pltpu.VMEM((2, tt, tt), F32),
            pltpu.VMEM((2, tt, tt), F32),
            pltpu.VMEM((2, tt, tt), BF16),
            pltpu.VMEM((2, tt, tt), BF16),
            pltpu.VMEM((d, tt), F32),
        ],
        compiler_params=_cparams(2),
        name="dsa_attn_ln",
    )(x3, qiT, wT, qlatT, kcat, ckv, ckvT, kcat_m, ckv_m, ckvT_m, wuvt, w_out, g, b)


def _row(v):
    return v.reshape(1, -1)


def kernel(x, meta, a_w_in, a_w_group, a_scale, a_w_out, b_w_in, b_w_uk, b_w_uv, b_w_out,
           c_w_in, c_conv, c_w_out, ln_mix_g, ln_mix_b, ffn_w_gu, ffn_w_down, ln_ffn_g, ln_ffn_b):
    bsz, seq, d = x.shape
    assert seq % DSA_TILE == 0 and meta.shape[0] == N_META
    topk = min(TOPK_MAX, seq // 4)
    lat = d // KV_LATENT_DIV
    row_tile = 2 * FFN_SUB_ROWS if (bsz * seq) % (2 * FFN_SUB_ROWS) == 0 else FFN_SUB_ROWS
    mixer_tile = 2 * MIXER_SUB_ROWS if seq % (2 * MIXER_SUB_ROWS) == 0 else MIXER_SUB_ROWS

    hx = x
    hm = meta.astype(x.dtype)

    for i in range(DEPTH):
        kind, jj = i % 3, i // 3
        g, b = _row(ln_mix_g[i]), _row(ln_mix_b[i])
        if kind == 0:
            args = (a_w_in[jj].astype(BF16), a_w_group[jj].astype(BF16), _row(a_scale[jj]),
                    a_w_out[jj].astype(BF16), g, b)
            hx_new = _pool_layer(hx, hm, *args, tq=mixer_tile, is_meta=False)
            hm = _pool_layer(hm[None], hm, *args, tq=N_META, is_meta=True)[0]
            hx = hx_new
        elif kind == 1:
            w = b_w_in[jj]
            o = 0
            wq = w[:, o:o + d].astype(BF16); o += d
            wckv = w[:, o:o + lat].astype(BF16); o += lat
            wqi = w[:, o:o + IDX_HEADS * IDX_DIM]; o += IDX_HEADS * IDX_DIM
            wk = w[:, o:o + IDX_DIM]; o += IDX_DIM
            ww = w[:, o:o + IDX_HEADS]
            wqi3t = jnp.tile(wqi.T.reshape(IDX_HEADS, 1, IDX_DIM, d), (1, 3, 1, 1))
            wqi3t = wqi3t.reshape(IDX_HEADS * 3 * IDX_DIM, d).astype(BF16)
            wk3 = jnp.tile(wk, (1, 3)).astype(BF16)
            wwt = ww.T.astype(BF16)
            wukt = jnp.swapaxes(b_w_uk[jj], 1, 2).astype(BF16)
            wuvt = jnp.swapaxes(b_w_uv[jj], 1, 2).astype(BF16)
            wout = b_w_out[jj].astype(BF16)
            pw = (wq, wukt, wckv, wqi3t, wk3, wwt)
            hm_pad = jnp.pad(hm, ((0, DSA_TILE - N_META), (0, 0)))[None]
            proj_m = _dsa_proj(hm_pad, *pw)
            proj_x = _dsa_proj(hx, *pw)
            hx_new = _dsa_attn(hx, proj_x, proj_m, wuvt, wout, g, b, meta_only=False, topk=topk)
            hm = _dsa_attn(hm_pad, proj_m, proj_m, wuvt, wout, g, b, meta_only=True,
                           topk=topk)[0, :N_META]
            hx = hx_new
        else:
            args = (c_w_in[jj].astype(BF16), c_conv[jj], c_w_out[jj].astype(BF16), g, b)
            hx_new = _conv_layer(hx, hm, *args, tq=mixer_tile, is_meta=False)
            hm = _conv_layer(hm[None], hm, *args, tq=N_META, is_meta=True)[0]
            hx = hx_new

        fargs = (ffn_w_gu[i].astype(BF16), ffn_w_down[i].astype(BF16),
                 _row(ln_ffn_g[i]), _row(ln_ffn_b[i]))
        hx = _ffn(hx.reshape(bsz * seq, d), *fargs, tile=row_tile).reshape(bsz, seq, d)
        if i + 1 < DEPTH:
            hm = _ffn(hm, *fargs, tile=N_META)
    return hx
```

```python
import functools

import jax
import jax.numpy as jnp
from jax import lax
from jax.experimental import pallas as pl
from jax.experimental.pallas import tpu as pltpu

CHUNK = 64
N_META = 16
POOL_WINDOWS = (2, 4, 8, 16)
N_HEADS = 8
KV_LATENT_DIV = 4
IDX_HEADS = 8
IDX_DIM = 64
TOPK_MAX = 256
CONV_WIDTH = 3
DEPTH = 4
ALPHA = (2.0 * DEPTH) ** 0.25
LN_EPS = 1e-5

BF16 = jnp.bfloat16
F32 = jnp.float32
I32 = jnp.int32

VMEM_LIMIT_BYTES = 56 * 1024 * 1024
DSA_TILE = 512
I16_ROWS = 16
INT_MIN = -(2 ** 31)
FLT_MAX = 3.4028234663852886e38
FLT_TINY = 1.1754943508222875e-38
COARSE_REL = 2.0 ** -6
MAX_SEARCH_ITERS = 32
COUNT_CHAINS = 4
SOFTMAX_CHAINS = 4
NEG_INF = float("-inf")
MAX_FLOOR = -1e30


def _cparams(n_axes):
    return pltpu.CompilerParams(
        dimension_semantics=("arbitrary",) * n_axes,
        vmem_limit_bytes=VMEM_LIMIT_BYTES,
    )


def _const_spec(shape):
    nd = len(shape)
    return pl.BlockSpec(shape, lambda *_: (0,) * nd, pipeline_mode=pl.Buffered(1))


def _layer_norm(z, g, b):
    mu = jnp.mean(z, axis=-1, keepdims=True)
    zc = z - mu
    var = jnp.mean(zc * zc, axis=-1, keepdims=True)
    return zc * lax.rsqrt(var + LN_EPS) * g + b


def _dot(a, b):
    return jnp.dot(a, b, preferred_element_type=F32)


def _dot_nt(a, b):
    return lax.dot_general(a, b, (((1,), (1,)), ((), ())), preferred_element_type=F32)


FFN_SUB_ROWS = 512

def _ffn_kernel(h_ref, wgu_ref, wd_ref, g_ref, b_ref, o_ref, *, d_ff, ff_chunk):
    rows = h_ref.shape[0]
    sub = min(rows, FFN_SUB_ROWS)
    for r0 in range(0, rows, sub):
        h = h_ref[r0:r0 + sub, :]
        hb = h.astype(BF16)
        acc = ALPHA * h
        for c in range(d_ff // ff_chunk):
            lo = c * ff_chunk
            gate = _dot(hb, wgu_ref[:, lo:lo + ff_chunk])
            up = _dot(hb, wgu_ref[:, d_ff + lo:d_ff + lo + ff_chunk])
            act = (gate * jax.nn.sigmoid(gate) * up).astype(BF16)
            acc = acc + _dot(act, wd_ref[lo:lo + ff_chunk, :])
        o_ref[r0:r0 + sub, :] = _layer_norm(acc, g_ref[...], b_ref[...])


def _ffn(h2d, w_gu, w_down, g, b, *, tile):
    n, d = h2d.shape
    d_ff = w_down.shape[0]
    ff_chunk = 256 if d_ff % 256 == 0 else d_ff
    return pl.pallas_call(
        functools.partial(_ffn_kernel, d_ff=d_ff, ff_chunk=ff_chunk),
        grid=(n // tile,),
        in_specs=[
            pl.BlockSpec((tile, d), lambda i: (i, 0)),
            _const_spec(w_gu.shape),
            _const_spec(w_down.shape),
            _const_spec(g.shape),
            _const_spec(b.shape),
        ],
        out_specs=pl.BlockSpec((tile, d), lambda i: (i, 0)),
        out_shape=jax.ShapeDtypeStruct((n, d), F32),
        compiler_params=_cparams(1),
        name="ffn_ln",
    )(h2d, w_gu, w_down, g, b)


POOL_HALO = 16
POOL_PAD = 8
MIXER_SUB_ROWS = 512


def _pool_kernel(x_ref, meta_ref, win_ref, wg_ref, sc_ref, wout_ref, g_ref, b_ref,
                 o_ref, ubuf, s2, s4, s8, *, tq, is_meta):
    j = pl.program_id(1)
    d = x_ref.shape[-1]
    gw = d // len(POOL_WINDOWS)
    sub = min(tq, MIXER_SUB_ROWS)
    pad, top = POOL_PAD, POOL_PAD + POOL_HALO

    ubuf[0:pad, :] = jnp.zeros((pad, d), F32)
    if is_meta:
        ubuf[pad:top, :] = jnp.zeros((POOL_HALO, d), F32)
    else:
        @pl.when(j == 0)
        def _():
            ubuf[pad:top, :] = _dot(meta_ref[...].astype(BF16), win_ref[...])

        @pl.when(j > 0)
        def _():
            ubuf[pad:top, :] = ubuf[pad + tq:top + tq, :]

    for si, r0 in enumerate(range(0, tq, sub)):
        x = x_ref[r0:r0 + sub, :]
        u = _dot(x.astype(BF16), win_ref[...])
        base = top + r0
        ubuf[base:base + sub, :] = u

        ext = sub + POOL_HALO
        e0 = base - POOL_HALO
        s2[si, 0:pad, :] = jnp.zeros((pad, d), F32)
        s2[si, pad:pad + ext, :] = ubuf[e0:e0 + ext, :] + ubuf[e0 - 1:e0 - 1 + ext, :]
        s4[si, 0:pad, :] = jnp.zeros((pad, d - gw), F32)
        s4[si, pad:pad + ext, :] = s2[si, pad:pad + ext, gw:] + s2[si, pad - 2:pad - 2 + ext, gw:]
        s8[si, :, :] = s4[si, pad:pad + ext, gw:] + s4[si, pad - 4:pad - 4 + ext, gw:]
        t0 = pad + POOL_HALO
        win_sums = (
            s2[si, t0:t0 + sub, 0:gw],
            s4[si, t0:t0 + sub, 0:gw],
            s8[si, POOL_HALO:POOL_HALO + sub, 0:gw],
            s8[si, POOL_HALO:POOL_HALO + sub, gw:] + s8[si, POOL_HALO - 8:POOL_HALO - 8 + sub, gw:],
        )

        ys = []
        for gi, w in enumerate(POOL_WINDOWS):
            c0 = gi * gw
            s = win_sums[gi]
            if is_meta:
                pos = lax.broadcasted_iota(I32, (sub, 1), 0) + 1
                cnt = jnp.minimum(pos, w).astype(F32)
                pooled = s / cnt
            else:
                pooled = s / float(w)
            dlt = (pooled - u[:, c0:c0 + gw]).astype(BF16)
            ys.append(_dot(dlt, wg_ref[gi]))
        y = jnp.concatenate(ys, axis=-1) * sc_ref[...]
        m = _dot(y.astype(BF16), wout_ref[...])
        o_ref[r0:r0 + sub, :] = _layer_norm(ALPHA * x + m, g_ref[...], b_ref[...])


def _pool_layer(x3, meta, w_in, w_group, scale, w_out, g, b, *, tq, is_meta):
    bsz, t, d = x3.shape
    assert POOL_WINDOWS == (2, 4, 8, 16)
    gw = d // len(POOL_WINDOWS)
    sub = min(tq, MIXER_SUB_ROWS)
    n_sub, ext = tq // sub, sub + POOL_HALO
    return pl.pallas_call(
        functools.partial(_pool_kernel, tq=tq, is_meta=is_meta),
        grid=(bsz, t // tq),
        in_specs=[
            pl.BlockSpec((None, tq, d), lambda i, j: (i, j, 0)),
            _const_spec(meta.shape),
            _const_spec(w_in.shape),
            _const_spec(w_group.shape),
            _const_spec(scale.shape),
            _const_spec(w_out.shape),
            _const_spec(g.shape),
            _const_spec(b.shape),
        ],
        out_specs=pl.BlockSpec((None, tq, d), lambda i, j: (i, j, 0)),
        out_shape=jax.ShapeDtypeStruct((bsz, t, d), F32),
        scratch_shapes=[
            pltpu.VMEM((POOL_PAD + POOL_HALO + tq, d), F32),
            pltpu.VMEM((n_sub, POOL_PAD + ext, d), F32),
            pltpu.VMEM((n_sub, POOL_PAD + ext, d - gw), F32),
            pltpu.VMEM((n_sub, ext, d - 2 * gw), F32),
        ],
        compiler_params=_cparams(2),
        name="pool_ln",
    )(x3, meta, w_in, w_group, scale, w_out, g, b)


CONV_HALO = 8


def _conv_kernel(x_ref, meta_ref, win_ref, cw_ref, wout_ref, g_ref, b_ref,
                 o_ref, zbuf, *, tq, is_meta):
    j = pl.program_id(1)
    d = x_ref.shape[-1]
    sub = min(tq, MIXER_SUB_ROWS)

    def gates(vb):
        bg = _dot(vb, win_ref[:, 0:d])
        cg = _dot(vb, win_ref[:, d:2 * d])
        hv = _dot(vb, win_ref[:, 2 * d:3 * d])
        return bg, cg * hv

    if is_meta:
        zbuf[0:CONV_HALO, :] = jnp.zeros((CONV_HALO, d), F32)
    else:
        @pl.when(j == 0)
        def _():
            _, zm = gates(meta_ref[...].astype(BF16))
            zbuf[0:CONV_HALO, :] = zm[N_META - CONV_HALO:N_META, :]

        @pl.when(j > 0)
        def _():
            zbuf[0:CONV_HALO, :] = zbuf[tq:tq + CONV_HALO, :]

    for r0 in range(0, tq, sub):
        x = x_ref[r0:r0 + sub, :]
        bg, z = gates(x.astype(BF16))
        base = CONV_HALO + r0
        zbuf[base:base + sub, :] = z
        conv = z * cw_ref[CONV_WIDTH - 1:CONV_WIDTH, :]
        for k in range(CONV_WIDTH - 1):
            sh = CONV_WIDTH - 1 - k
            conv = conv + zbuf[base - sh:base - sh + sub, :] * cw_ref[k:k + 1, :]
        m = _dot((bg * conv).astype(BF16), wout_ref[...])
        o_ref[r0:r0 + sub, :] = _layer_norm(ALPHA * x + m, g_ref[...], b_ref[...])


def _conv_layer(x3, meta, w_in, conv_w, w_out, g, b, *, tq, is_meta):
    bsz, t, d = x3.shape
    return pl.pallas_call(
        functools.partial(_conv_kernel, tq=tq, is_meta=is_meta),
        grid=(bsz, t // tq),
        in_specs=[
            pl.BlockSpec((None, tq, d), lambda i, j: (i, j, 0)),
            _const_spec(meta.shape),
            _const_spec(w_in.shape),
            _const_spec(conv_w.shape),
            _const_spec(w_out.shape),
            _const_spec(g.shape),
            _const_spec(b.shape),
        ],
        out_specs=pl.BlockSpec((None, tq, d), lambda i, j: (i, j, 0)),
        out_shape=jax.ShapeDtypeStruct((bsz, t, d), F32),
        scratch_shapes=[pltpu.VMEM((tq + CONV_HALO, d), F32)],
        compiler_params=_cparams(2),
        name="conv_ln",
    )(x3, meta, w_in, conv_w, w_out, g, b)


def _split_hi_lo(v):
    hi = v.astype(BF16)
    lo = (v - hi.astype(F32)).astype(BF16)
    return hi, lo


def _dsa_proj_kernel(x_ref, wq_ref, wukt_ref, wckv_ref, wqi3t_ref, wk3_ref, wwt_ref,
                     qlatT_ref, ckv_ref, ckvT_ref, qiT_ref, kcat_ref, wT_ref):
    xb = x_ref[...].astype(BF16)
    hd = wukt_ref.shape[2]
    q = _dot(xb, wq_ref[...])
    for h in range(N_HEADS):
        qh = q[:, h * hd:(h + 1) * hd].astype(BF16)
        qlatT_ref[h] = _dot_nt(wukt_ref[h], qh).astype(BF16)
    ckv = _dot(xb, wckv_ref[...])
    ckv_ref[...] = ckv.astype(BF16)
    ckvT_ref[...] = ckv.T.astype(BF16)

    qi3 = _dot_nt(wqi3t_ref[...], xb)
    hi, lo = _split_hi_lo(qi3)
    row = lax.broadcasted_iota(I32, qi3.shape, 0) % (3 * IDX_DIM)
    is_lo = (row >= IDX_DIM) & (row < 2 * IDX_DIM)
    qsel = jnp.where(is_lo, lo, hi)
    for h in range(IDX_HEADS):
        qiT_ref[h] = qsel[h * 3 * IDX_DIM:(h + 1) * 3 * IDX_DIM, :]
    k3 = _dot(xb, wk3_ref[...])
    khi, klo = _split_hi_lo(k3)
    col = lax.broadcasted_iota(I32, k3.shape, 1)
    kcat_ref[...] = jnp.where(col >= 2 * IDX_DIM, klo, khi)
    wT_ref[...] = _dot_nt(wwt_ref[...], xb)


def _dsa_proj(x3, wq, wukt, wckv, wqi3t, wk3, wwt):
    bsz, t, d = x3.shape
    tt = DSA_TILE
    nt = t // tt
    lat = wckv.shape[1]
    outs = pl.pallas_call(
        _dsa_proj_kernel,
        grid=(bsz, nt),
        in_specs=[
            pl.BlockSpec((None, tt, d), lambda i, j: (i, j, 0)),
            _const_spec(wq.shape),
            _const_spec(wukt.shape),
            _const_spec(wckv.shape),
            _const_spec(wqi3t.shape),
            _const_spec(wk3.shape),
            _const_spec(wwt.shape),
        ],
        out_specs=[
            pl.BlockSpec((None, N_HEADS, lat, tt), lambda i, j: (i, 0, 0, j)),
            pl.BlockSpec((None, tt, lat), lambda i, j: (i, j, 0)),
            pl.BlockSpec((None, None, lat, tt), lambda i, j: (i, j, 0, 0)),
            pl.BlockSpec((None, IDX_HEADS, 3 * IDX_DIM, tt), lambda i, j: (i, 0, 0, j)),
            pl.BlockSpec((None, tt, 3 * IDX_DIM), lambda i, j: (i, j, 0)),
            pl.BlockSpec((None, IDX_HEADS, tt), lambda i, j: (i, 0, j)),
        ],
        out_shape=[
            jax.ShapeDtypeStruct((bsz, N_HEADS, lat, t), BF16),
            jax.ShapeDtypeStruct((bsz, t, lat), BF16),
            jax.ShapeDtypeStruct((bsz, nt, lat, tt), BF16),
            jax.ShapeDtypeStruct((bsz, IDX_HEADS, 3 * IDX_DIM, t), BF16),
            jax.ShapeDtypeStruct((bsz, t, 3 * IDX_DIM), BF16),
            jax.ShapeDtypeStruct((bsz, IDX_HEADS, t), F32),
        ],
        compiler_params=_cparams(2),
        name="dsa_proj",
    )(x3, wq, wukt, wckv, wqi3t, wk3, wwt)
    return outs


def _dsa_attn_kernel(x_ref, qiT_ref, wT_ref, qlatT_ref, kcat_ref, ckv_ref, ckvT_ref,
                     kcat_m_ref, ckv_m_ref, ckvT_m_ref, wuvt_ref, wout_ref, g_ref, b_ref,
                     o_ref, scores, scores_m, s16, s16_m, thr_s, acc_s, lg_a, lg_b, p_a, p_b, outT,
                     *, meta_only, topk, head_dim):
    tt = DSA_TILE
    j = pl.program_id(1)
    nx = 0 if meta_only else j + 1

    def tile_rows(t):
        return pl.ds(pl.multiple_of(t * tt, tt), tt)

    def raw_scores(kc):
        acc = jnp.zeros((kc.shape[0], tt), F32)
        for h in range(IDX_HEADS):
            s = _dot(kc, qiT_ref[h])
            acc = acc + jnp.maximum(s, 0.0) * wT_ref[h:h + 1, :]
        return acc

    sm = raw_scores(kcat_m_ref[...])
    scores_m[...] = sm
    s16_m[...] = sm.astype(BF16)

    def score_body(t, carry):
        mx, mn = carry
        row = lax.broadcasted_iota(I32, (tt, tt), 0)
        col = lax.broadcasted_iota(I32, (tt, tt), 1)
        kchunk = t * (tt // CHUNK) + row // CHUNK
        qchunk = j * (tt // CHUNK) + col // CHUNK
        acc = raw_scores(kcat_ref[tile_rows(t), :])
        masked = jnp.where(kchunk <= qchunk, acc, NEG_INF)
        scores[t] = masked
        s16[t] = masked.astype(BF16)
        return (jnp.maximum(mx, acc.max(axis=0, keepdims=True)),
                jnp.minimum(mn, acc.min(axis=0, keepdims=True)))

    mx, mn = lax.fori_loop(0, nx, score_body, (sm.max(axis=0, keepdims=True),
                                               sm.min(axis=0, keepdims=True)))

    def chained_count(hits, buf, buf_m, rows):
        def body(t, accs):
            accs = list(accs)
            for i in range(tt // rows):
                c = i % COUNT_CHAINS
                accs[c] = accs[c] + hits(buf[t, i * rows:(i + 1) * rows, :])
            return tuple(accs)

        init = [hits(buf_m[i * rows:(i + 1) * rows, :]) for i in range(N_META // rows)]
        init = tuple(init + [jnp.zeros_like(init[0])] * (COUNT_CHAINS - len(init)))
        accs = lax.fori_loop(0, nx, body, init)
        total = accs[0].astype(F32)
        for a in accs[1:]:
            total = total + a.astype(F32)
        return total.sum(axis=0, keepdims=True)

    def count(thr, strict=False):
        return chained_count(lambda v: jnp.where((v > thr) if strict else (v >= thr), 1.0, 0.0),
                             scores, scores_m, 8)

    def count16(thr):
        t16 = thr.astype(BF16)
        hit, miss = jnp.ones((I16_ROWS, tt), BF16), jnp.zeros((I16_ROWS, tt), BF16)
        return chained_count(lambda v: jnp.where(v >= t16, hit, miss), s16, s16_m, I16_ROWS)

    kf = float(topk)
    one, zero = jnp.ones((1, tt), F32), jnp.zeros((1, tt), F32)

    def n_open(done, tie0, stuck):
        return jnp.sum(jnp.where((done + tie0 + stuck) == 0.0, 1.0, 0.0))

    def widen(v, rel, sign):
        return v + sign * (jnp.abs(v) * rel + 2.0 * FLT_TINY)

    def search(count_fn, round_fn, probe_zero, lo, hi, glo, ghi, done, tstar):
        def cond(c):
            return jnp.logical_and(c[0] < MAX_SEARCH_ITERS, c[1] > 0.0)

        def step(c):
            it, _, lo, hi, glo, ghi, done, tie0, stuck, last, tstar = c
            active = (done + tie0 + stuck) == 0.0
            cand = round_fn(lo + (hi - lo) * (glo / (glo - ghi)))
            cand = jnp.where((cand > lo) & (cand < hi), cand, round_fn(0.5 * lo + 0.5 * hi))
            zero_probe = jnp.logical_and(it == 0, lo == 0.0) if probe_zero else (lo != lo)
            cand = jnp.where(zero_probe, FLT_TINY, cand)
            inside = (cand > lo) & (cand < hi)
            g = count_fn(cand) - kf
            ok = active & inside
            hit = ok & (g == 0.0)
            up = ok & (g > 0.0)
            dn = ok & (g < 0.0)
            glo_n = jnp.where(up, g, jnp.where(dn & (last < 0.0), 0.5 * glo, glo))
            ghi_n = jnp.where(dn, g, jnp.where(up & (last > 0.0), 0.5 * ghi, ghi))
            done = jnp.where(hit, 1.0, done)
            tie0 = jnp.where(dn & zero_probe, 1.0, tie0)
            stuck = jnp.where(active & jnp.logical_not(inside), 1.0, stuck)
            return (it + 1, n_open(done, tie0, stuck), jnp.where(up, cand, lo),
                    jnp.where(dn, cand, hi), glo_n, ghi_n, done, tie0, stuck,
                    jnp.where(up, 1.0, jnp.where(dn, -1.0, last)), jnp.where(hit, cand, tstar))

        res = lax.while_loop(cond, step, (jnp.int32(0), n_open(done, zero, zero), lo, hi, glo, ghi,
                                          done, zero, zero, zero, tstar))
        return res[2], res[3], res[6], res[7], res[10]

    if meta_only:
        n_valid = jnp.full((1, tt), float(N_META), F32)
    else:
        qpos = j * tt + lax.broadcasted_iota(I32, (1, tt), 1)
        n_valid = (N_META + CHUNK * (qpos // CHUNK + 1)).astype(F32)
    all_sel = n_valid <= kf

    c0 = count16(zero)
    pos = c0 > kf
    lo_a, hi_a, done_a, tie0, t_a = search(
        count16, lambda v: v.astype(BF16).astype(F32), True,
        jnp.where(pos, zero, widen(mn, COARSE_REL, -1.0)),
        jnp.where(pos, widen(mx, COARSE_REL, 1.0), zero),
        jnp.where(pos, c0 - kf, n_valid - kf), jnp.where(pos, -kf, c0 - kf),
        jnp.where(all_sel | (c0 == kf), one, zero), zero)

    settled = all_sel | (tie0 > 0.0)
    lo_b = widen(jnp.where(done_a > 0.0, t_a, lo_a), COARSE_REL, -1.0)
    hi_b = widen(jnp.where(done_a > 0.0, t_a, hi_a), COARSE_REL, 1.0)
    glo = count(lo_b) - kf
    ghi = count(hi_b) - kf
    at_end = (glo == 0.0) | (ghi == 0.0)
    bad = jnp.logical_not(settled | at_end | ((glo > 0.0) & (ghi < 0.0)))
    t_b = jnp.where(all_sel, -FLT_MAX, jnp.where(ghi == 0.0, hi_b, jnp.where(glo == 0.0, lo_b, zero)))
    _, _, done, _, tstar = search(
        count, lambda v: v, False, lo_b, hi_b, glo, ghi,
        jnp.where(settled | at_end | bad, one, zero), t_b)
    resolved = jnp.logical_and(jnp.logical_not(bad), (done + tie0) > 0.0)
    thr_s[0:1, :] = jnp.where(tie0 > 0.0, 0.0, tstar)
    thr_s[1:2, :] = tie0

    @pl.when(jnp.sum(jnp.where(resolved, 0.0, 1.0)) > 0.0)
    def _():
        def int_key(v):
            bits = pltpu.bitcast(jnp.where(v == 0.0, 0.0, v), I32)
            return jnp.where(bits < 0, bits ^ 0x7FFFFFFF, bits)

        def bisect(i, kth):
            cand = kth + lax.shift_left(jnp.int32(1), 31 - i)
            cnt = chained_count(lambda v: jnp.where(int_key(v) >= cand, 1, 0), scores, scores_m, 8)
            return jnp.where(cnt >= topk, cand, kth)

        kth = lax.fori_loop(0, 32, bisect, jnp.full((1, tt), INT_MIN, I32))
        t_fb = pltpu.bitcast(jnp.where(kth < 0, kth ^ 0x7FFFFFFF, kth), F32)
        ties_fb = jnp.where(count(t_fb) > kf, 1.0, 0.0)
        thr_s[0:1, :] = jnp.where(resolved, thr_s[0:1, :], t_fb)
        thr_s[1:2, :] = jnp.where(resolved, thr_s[1:2, :], ties_fb)

    thr = thr_s[0:1, :]
    has_ties = jnp.sum(thr_s[1:2, :]) > 0.0

    @pl.when(jnp.logical_not(has_ties))
    def _():
        scores_m[...] = jnp.where(scores_m[...] >= thr, 0.0, NEG_INF)

        def body(t, carry):
            scores[t] = jnp.where(scores[t] >= thr, 0.0, NEG_INF)
            return carry

        lax.fori_loop(0, nx, body, 0)

    @pl.when(has_ties)
    def _():
        need = kf - count(thr, strict=True)

        def select(v, before):
            eq = v == thr
            sel = (v > thr) | (eq & (before < need))
            return jnp.where(sel, 0.0, NEG_INF)

        def strictly_lower(n):
            r = lax.broadcasted_iota(I32, (n, n), 0)
            c = lax.broadcasted_iota(I32, (n, n), 1)
            return jnp.where(c < r, 1.0, 0.0).astype(BF16)

        vm = scores_m[...]
        eqm = jnp.where(vm == thr, 1.0, 0.0)
        scores_m[...] = select(vm, _dot(strictly_lower(N_META), eqm.astype(BF16)))
        ltri = strictly_lower(tt)

        def body(t, taken):
            v = scores[t]
            eqf = jnp.where(v == thr, 1.0, 0.0)
            scores[t] = select(v, _dot(ltri, eqf.astype(BF16)) + taken)
            return taken + eqf.sum(axis=0, keepdims=True)

        lax.fori_loop(0, nx, body, eqm.sum(axis=0, keepdims=True))

    c_exp = (head_dim ** -0.5) * 1.4426950408889634

    bias_m = scores_m[...]
    ms, ls = [], []
    for h in range(N_HEADS):
        lg = _dot(ckv_m_ref[...], qlatT_ref[h]) + bias_m
        m = jnp.maximum(lg.max(axis=0, keepdims=True), MAX_FLOOR)
        p = jnp.exp2((lg - m) * c_exp)
        ms.append(m)
        ls.append(p.sum(axis=0, keepdims=True))
        acc_s[h] = _dot(ckvT_m_ref[...], p.astype(BF16))

    def tile_step(t, carry):
        m_all, l_all = carry
        kv = ckv_ref[tile_rows(t), :]
        kvT = ckvT_ref[t]
        ms, ls = [], []

        def tree(vals, op):
            while len(vals) > 1:
                vals = [op(a, b) for a, b in zip(vals[0::2], vals[1::2])]
            return vals[0]

        par = t & 1

        def biased_max(lg):
            accs = [None] * SOFTMAX_CHAINS
            for i in range(tt // 8):
                rows = slice(i * 8, (i + 1) * 8)
                x = lg[par, rows, :] + scores[t, rows, :]
                lg[par, rows, :] = x
                c = i % SOFTMAX_CHAINS
                accs[c] = x if accs[c] is None else jnp.maximum(accs[c], x)
            return tree(accs, jnp.maximum).max(axis=0, keepdims=True)

        def exp_chunks(lg, ps, m_new):
            accs = [None] * SOFTMAX_CHAINS
            for i in range(tt // I16_ROWS):
                rows = slice(i * I16_ROWS, (i + 1) * I16_ROWS)
                p = jnp.exp2((lg[par, rows, :] - m_new) * c_exp)
                ps[par, rows, :] = p.astype(BF16)
                c = i % SOFTMAX_CHAINS
                accs[c] = p if accs[c] is None else accs[c] + p
            return tree(accs, jnp.add).sum(axis=0, keepdims=True)

        lgs, pss = (lg_a, lg_b), (p_a, p_b)
        lgs[0][par] = _dot(kv, qlatT_ref[0])
        for h in range(N_HEADS):
            if h + 1 < N_HEADS:
                lgs[(h + 1) % 2][par] = _dot(kv, qlatT_ref[h + 1])
            lg, ps = lgs[h % 2], pss[h % 2]
            m_old = m_all[h:h + 1, :]
            m_new = jnp.maximum(m_old, biased_max(lg))
            a = jnp.exp2((m_old - m_new) * c_exp)
            l_tile = exp_chunks(lg, ps, m_new)
            ms.append(m_new)
            ls.append(a * l_all[h:h + 1, :] + l_tile)
            acc_s[h] = a * acc_s[h] + _dot(kvT, ps[par])
        return jnp.concatenate(ms, axis=0), jnp.concatenate(ls, axis=0)

    _, l_all = lax.fori_loop(0, nx, tile_step,
                             (jnp.concatenate(ms, axis=0), jnp.concatenate(ls, axis=0)))

    hd = wuvt_ref.shape[1]
    inv_l = 1.0 / l_all
    for h in range(N_HEADS):
        ctx = (acc_s[h] * inv_l[h:h + 1, :]).astype(BF16)
        outT[h * hd:(h + 1) * hd, :] = _dot(wuvt_ref[h], ctx)

    out = outT[...].T.astype(BF16)
    mix = _dot(out, wout_ref[...])
    o_ref[...] = _layer_norm(ALPHA * x_ref[...] + mix, g_ref[...], b_ref[...])


def _dsa_attn(x3, proj, proj_meta, wuvt, w_out, g, b, *, meta_only, topk):
    bsz, t, d = x3.shape
    tt = DSA_TILE
    nt = t // tt
    qlatT, ckv, ckvT, qiT, kcat, wT = proj
    _, ckv_m, ckvT_m, _, kcat_m, _ = proj_meta
    lat = ckv.shape[-1]
    kcat_m, ckv_m, ckvT_m = kcat_m[0, :N_META], ckv_m[0, :N_META], ckvT_m[0, 0, :, :N_META]
    head_dim = wuvt.shape[1]
    return pl.pallas_call(
        functools.partial(_dsa_attn_kernel, meta_only=meta_only, topk=topk, head_dim=head_dim),
        grid=(bsz, nt),
        in_specs=[
            pl.BlockSpec((None, tt, d), lambda i, j: (i, j, 0)),
            pl.BlockSpec((None, IDX_HEADS, 3 * IDX_DIM, tt), lambda i, j: (i, 0, 0, j)),
            pl.BlockSpec((None, IDX_HEADS, tt), lambda i, j: (i, 0, j)),
            pl.BlockSpec((None, N_HEADS, lat, tt), lambda i, j: (i, 0, 0, j)),
            pl.BlockSpec((None, t, 3 * IDX_DIM), lambda i, j: (i, 0, 0),
                         pipeline_mode=pl.Buffered(1)),
            pl.BlockSpec((None, t, lat), lambda i, j: (i, 0, 0),
                         pipeline_mode=pl.Buffered(1)),
            pl.BlockSpec((None, nt, lat, tt), lambda i, j: (i, 0, 0, 0),
                         pipeline_mode=pl.Buffered(1)),
            _const_spec(kcat_m.shape),
            _const_spec(ckv_m.shape),
            _const_spec(ckvT_m.shape),
            _const_spec(wuvt.shape),
            _const_spec(w_out.shape),
            _const_spec(g.shape),
            _const_spec(b.shape),
        ],
        out_specs=pl.BlockSpec((None, tt, d), lambda i, j: (i, j, 0)),
        out_shape=jax.ShapeDtypeStruct((bsz, t, d), F32),
        scratch_shapes=[
            pltpu.VMEM((nt, tt, tt), F32),
            pltpu.VMEM((N_META, tt), F32),
            pltpu.VMEM((nt, tt, tt), BF16),
            pltpu.VMEM((N_META, tt), BF16),
            pltpu.VMEM((8, tt), F32),
            pltpu.VMEM((N_HEADS, lat, tt), F32),
            pltpu.VMEM((2, tt, tt), F32),
            pltpu.VMEM((2, tt, tt), F32),
            pltpu.VMEM((2, tt, tt), BF16),
            pltpu.VMEM((2, tt, tt), BF16),
            pltpu.VMEM((d, tt), F32),
        ],
        compiler_params=_cparams(2),
        name="dsa_attn_ln",
    )(x3, qiT, wT, qlatT, kcat, ckv, ckvT, kcat_m, ckv_m, ckvT_m, wuvt, w_out, g, b)


def _row(v):
    return v.reshape(1, -1)


def kernel(x, meta, a_w_in, a_w_group, a_scale, a_w_out, b_w_in, b_w_uk, b_w_uv, b_w_out,
           c_w_in, c_conv, c_w_out, ln_mix_g, ln_mix_b, ffn_w_gu, ffn_w_down, ln_ffn_g, ln_ffn_b):
    bsz, seq, d = x.shape
    assert seq % DSA_TILE == 0 and meta.shape[0] == N_META
    topk = min(TOPK_MAX, seq // 4)
    lat = d // KV_LATENT_DIV
    row_tile = 2 * FFN_SUB_ROWS if (bsz * seq) % (2 * FFN_SUB_ROWS) == 0 else FFN_SUB_ROWS
    mixer_tile = 2 * MIXER_SUB_ROWS if seq % (2 * MIXER_SUB_ROWS) == 0 else MIXER_SUB_ROWS

    hx = x
    hm = meta.astype(x.dtype)

    for i in range(DEPTH):
        kind, jj = i % 3, i // 3
        g, b = _row(ln_mix_g[i]), _row(ln_mix_b[i])
        if kind == 0:
            args = (a_w_in[jj].astype(BF16), a_w_group[jj].astype(BF16), _row(a_scale[jj]),
                    a_w_out[jj].astype(BF16), g, b)
            hx_new = _pool_layer(hx, hm, *args, tq=mixer_tile, is_meta=False)
            hm = _pool_layer(hm[None], hm, *args, tq=N_META, is_meta=True)[0]
            hx = hx_new
        elif kind == 1:
            w = b_w_in[jj]
            o = 0
            wq = w[:, o:o + d].astype(BF16); o += d
            wckv = w[:, o:o + lat].astype(BF16); o += lat
            wqi = w[:, o:o + IDX_HEADS * IDX_DIM]; o += IDX_HEADS * IDX_DIM
            wk = w[:, o:o + IDX_DIM]; o += IDX_DIM
            ww = w[:, o:o + IDX_HEADS]
            wqi3t = jnp.tile(wqi.T.reshape(IDX_HEADS, 1, IDX_DIM, d), (1, 3, 1, 1))
            wqi3t = wqi3t.reshape(IDX_HEADS * 3 * IDX_DIM, d).astype(BF16)
            wk3 = jnp.tile(wk, (1, 3)).astype(BF16)
            wwt = ww.T.astype(BF16)
            wukt = jnp.swapaxes(b_w_uk[jj], 1, 2).astype(BF16)
            wuvt = jnp.swapaxes(b_w_uv[jj], 1, 2).astype(BF16)
            wout = b_w_out[jj].astype(BF16)
            pw = (wq, wukt, wckv, wqi3t, wk3, wwt)
            hm_pad = jnp.pad(hm, ((0, DSA_TILE - N_META), (0, 0)))[None]
            proj_m = _dsa_proj(hm_pad, *pw)
            proj_x = _dsa_proj(hx, *pw)
            hx_new = _dsa_attn(hx, proj_x, proj_m, wuvt, wout, g, b, meta_only=False, topk=topk)
            hm = _dsa_attn(hm_pad, proj_m, proj_m, wuvt, wout, g, b, meta_only=True,
                           topk=topk)[0, :N_META]
            hx = hx_new
        else:
            args = (c_w_in[jj].astype(BF16), c_conv[jj], c_w_out[jj].astype(BF16), g, b)
            hx_new = _conv_layer(hx, hm, *args, tq=mixer_tile, is_meta=False)
            hm = _conv_layer(hm[None], hm, *args, tq=N_META, is_meta=True)[0]
            hx = hx_new

        fargs = (ffn_w_gu[i].astype(BF16), ffn_w_down[i].astype(BF16),
                 _row(ln_ffn_g[i]), _row(ln_ffn_b[i]))
        hx = _ffn(hx.reshape(bsz * seq, d), *fargs, tile=row_tile).reshape(bsz, seq, d)
        if i + 1 < DEPTH:
            hm = _ffn(hm, *fargs, tile=N_META)
    return hx
```

```python
import functools

import jax
import jax.numpy as jnp
from jax import lax
from jax.experimental import pallas as pl
from jax.experimental.pallas import tpu as pltpu

CHUNK = 64
N_META = 16
POOL_WINDOWS = (2, 4, 8, 16)
N_HEADS = 8
KV_LATENT_DIV = 4
IDX_HEADS = 8
IDX_DIM = 64
TOPK_MAX = 256
CONV_WIDTH = 3
DEPTH = 4
ALPHA = (2.0 * DEPTH) ** 0.25
LN_EPS = 1e-5

BF16 = jnp.bfloat16
F32 = jnp.float32
I32 = jnp.int32

VMEM_LIMIT_BYTES = 56 * 1024 * 1024
DSA_TILE = 512
I16_ROWS = 16
INT_MIN = -(2 ** 31)
FLT_MAX = 3.4028234663852886e38
FLT_TINY = 1.1754943508222875e-38
BRACKET_REL = 1e-6
MAX_SEARCH_ITERS = 40
COUNT_CHAINS = 4
SOFTMAX_CHAINS = 4
NEG_INF = float("-inf")
MAX_FLOOR = -1e30


def _cparams(n_axes):
    return pltpu.CompilerParams(
        dimension_semantics=("arbitrary",) * n_axes,
        vmem_limit_bytes=VMEM_LIMIT_BYTES,
    )


def _const_spec(shape):
    nd = len(shape)
    return pl.BlockSpec(shape, lambda *_: (0,) * nd, pipeline_mode=pl.Buffered(1))


def _layer_norm(z, g, b):
    mu = jnp.mean(z, axis=-1, keepdims=True)
    zc = z - mu
    var = jnp.mean(zc * zc, axis=-1, keepdims=True)
    return zc * lax.rsqrt(var + LN_EPS) * g + b


def _dot(a, b):
    return jnp.dot(a, b, preferred_element_type=F32)


def _dot_nt(a, b):
    return lax.dot_general(a, b, (((1,), (1,)), ((), ())), preferred_element_type=F32)


FFN_SUB_ROWS = 512

def _ffn_kernel(h_ref, wgu_ref, wd_ref, g_ref, b_ref, o_ref, *, d_ff, ff_chunk):
    rows = h_ref.shape[0]
    sub = min(rows, FFN_SUB_ROWS)
    for r0 in range(0, rows, sub):
        h = h_ref[r0:r0 + sub, :]
        hb = h.astype(BF16)
        acc = ALPHA * h
        for c in range(d_ff // ff_chunk):
            lo = c * ff_chunk
            gate = _dot(hb, wgu_ref[:, lo:lo + ff_chunk])
            up = _dot(hb, wgu_ref[:, d_ff + lo:d_ff + lo + ff_chunk])
            act = (gate * jax.nn.sigmoid(gate) * up).astype(BF16)
            acc = acc + _dot(act, wd_ref[lo:lo + ff_chunk, :])
        o_ref[r0:r0 + sub, :] = _layer_norm(acc, g_ref[...], b_ref[...])


def _ffn(h2d, w_gu, w_down, g, b, *, tile):
    n, d = h2d.shape
    d_ff = w_down.shape[0]
    ff_chunk = 256 if d_ff % 256 == 0 else d_ff
    return pl.pallas_call(
        functools.partial(_ffn_kernel, d_ff=d_ff, ff_chunk=ff_chunk),
        grid=(n // tile,),
        in_specs=[
            pl.BlockSpec((tile, d), lambda i: (i, 0)),
            _const_spec(w_gu.shape),
            _const_spec(w_down.shape),
            _const_spec(g.shape),
            _const_spec(b.shape),
        ],
        out_specs=pl.BlockSpec((tile, d), lambda i: (i, 0)),
        out_shape=jax.ShapeDtypeStruct((n, d), F32),
        compiler_params=_cparams(1),
        name="ffn_ln",
    )(h2d, w_gu, w_down, g, b)


POOL_HALO = 16
POOL_PAD = 8
MIXER_SUB_ROWS = 512


def _pool_kernel(x_ref, meta_ref, win_ref, wg_ref, sc_ref, wout_ref, g_ref, b_ref,
                 o_ref, ubuf, s2, s4, s8, *, tq, is_meta):
    j = pl.program_id(1)
    d = x_ref.shape[-1]
    gw = d // len(POOL_WINDOWS)
    sub = min(tq, MIXER_SUB_ROWS)
    pad, top = POOL_PAD, POOL_PAD + POOL_HALO

    ubuf[0:pad, :] = jnp.zeros((pad, d), F32)
    if is_meta:
        ubuf[pad:top, :] = jnp.zeros((POOL_HALO, d), F32)
    else:
        @pl.when(j == 0)
        def _():
            ubuf[pad:top, :] = _dot(meta_ref[...].astype(BF16), win_ref[...])

        @pl.when(j > 0)
        def _():
            ubuf[pad:top, :] = ubuf[pad + tq:top + tq, :]

    for si, r0 in enumerate(range(0, tq, sub)):
        x = x_ref[r0:r0 + sub, :]
        u = _dot(x.astype(BF16), win_ref[...])
        base = top + r0
        ubuf[base:base + sub, :] = u

        ext = sub + POOL_HALO
        e0 = base - POOL_HALO
        s2[si, 0:pad, :] = jnp.zeros((pad, d), F32)
        s2[si, pad:pad + ext, :] = ubuf[e0:e0 + ext, :] + ubuf[e0 - 1:e0 - 1 + ext, :]
        s4[si, 0:pad, :] = jnp.zeros((pad, d - gw), F32)
        s4[si, pad:pad + ext, :] = s2[si, pad:pad + ext, gw:] + s2[si, pad - 2:pad - 2 + ext, gw:]
        s8[si, :, :] = s4[si, pad:pad + ext, gw:] + s4[si, pad - 4:pad - 4 + ext, gw:]
        t0 = pad + POOL_HALO
        win_sums = (
            s2[si, t0:t0 + sub, 0:gw],
            s4[si, t0:t0 + sub, 0:gw],
            s8[si, POOL_HALO:POOL_HALO + sub, 0:gw],
            s8[si, POOL_HALO:POOL_HALO + sub, gw:] + s8[si, POOL_HALO - 8:POOL_HALO - 8 + sub, gw:],
        )

        ys = []
        for gi, w in enumerate(POOL_WINDOWS):
            c0 = gi * gw
            s = win_sums[gi]
            if is_meta:
                pos = lax.broadcasted_iota(I32, (sub, 1), 0) + 1
                cnt = jnp.minimum(pos, w).astype(F32)
                pooled = s / cnt
            else:
                pooled = s / float(w)
            dlt = (pooled - u[:, c0:c0 + gw]).astype(BF16)
            ys.append(_dot(dlt, wg_ref[gi]))
        y = jnp.concatenate(ys, axis=-1) * sc_ref[...]
        m = _dot(y.astype(BF16), wout_ref[...])
        o_ref[r0:r0 + sub, :] = _layer_norm(ALPHA * x + m, g_ref[...], b_ref[...])


def _pool_layer(x3, meta, w_in, w_group, scale, w_out, g, b, *, tq, is_meta):
    bsz, t, d = x3.shape
    assert POOL_WINDOWS == (2, 4, 8, 16)
    gw = d // len(POOL_WINDOWS)
    sub = min(tq, MIXER_SUB_ROWS)
    n_sub, ext = tq // sub, sub + POOL_HALO
    return pl.pallas_call(
        functools.partial(_pool_kernel, tq=tq, is_meta=is_meta),
        grid=(bsz, t // tq),
        in_specs=[
            pl.BlockSpec((None, tq, d), lambda i, j: (i, j, 0)),
            _const_spec(meta.shape),
            _const_spec(w_in.shape),
            _const_spec(w_group.shape),
            _const_spec(scale.shape),
            _const_spec(w_out.shape),
            _const_spec(g.shape),
            _const_spec(b.shape),
        ],
        out_specs=pl.BlockSpec((None, tq, d), lambda i, j: (i, j, 0)),
        out_shape=jax.ShapeDtypeStruct((bsz, t, d), F32),
        scratch_shapes=[
            pltpu.VMEM((POOL_PAD + POOL_HALO + tq, d), F32),
            pltpu.VMEM((n_sub, POOL_PAD + ext, d), F32),
            pltpu.VMEM((n_sub, POOL_PAD + ext, d - gw), F32),
            pltpu.VMEM((n_sub, ext, d - 2 * gw), F32),
        ],
        compiler_params=_cparams(2),
        name="pool_ln",
    )(x3, meta, w_in, w_group, scale, w_out, g, b)


CONV_HALO = 8


def _conv_kernel(x_ref, meta_ref, win_ref, cw_ref, wout_ref, g_ref, b_ref,
                 o_ref, zbuf, *, tq, is_meta):
    j = pl.program_id(1)
    d = x_ref.shape[-1]
    sub = min(tq, MIXER_SUB_ROWS)

    def gates(vb):
        bg = _dot(vb, win_ref[:, 0:d])
        cg = _dot(vb, win_ref[:, d:2 * d])
        hv = _dot(vb, win_ref[:, 2 * d:3 * d])
        return bg, cg * hv

    if is_meta:
        zbuf[0:CONV_HALO, :] = jnp.zeros((CONV_HALO, d), F32)
    else:
        @pl.when(j == 0)
        def _():
            _, zm = gates(meta_ref[...].astype(BF16))
            zbuf[0:CONV_HALO, :] = zm[N_META - CONV_HALO:N_META, :]

        @pl.when(j > 0)
        def _():
            zbuf[0:CONV_HALO, :] = zbuf[tq:tq + CONV_HALO, :]

    for r0 in range(0, tq, sub):
        x = x_ref[r0:r0 + sub, :]
        bg, z = gates(x.astype(BF16))
        base = CONV_HALO + r0
        zbuf[base:base + sub, :] = z
        conv = z * cw_ref[CONV_WIDTH - 1:CONV_WIDTH, :]
        for k in range(CONV_WIDTH - 1):
            sh = CONV_WIDTH - 1 - k
            conv = conv + zbuf[base - sh:base - sh + sub, :] * cw_ref[k:k + 1, :]
        m = _dot((bg * conv).astype(BF16), wout_ref[...])
        o_ref[r0:r0 + sub, :] = _layer_norm(ALPHA * x + m, g_ref[...], b_ref[...])


def _conv_layer(x3, meta, w_in, conv_w, w_out, g, b, *, tq, is_meta):
    bsz, t, d = x3.shape
    return pl.pallas_call(
        functools.partial(_conv_kernel, tq=tq, is_meta=is_meta),
        grid=(bsz, t // tq),
        in_specs=[
            pl.BlockSpec((None, tq, d), lambda i, j: (i, j, 0)),
            _const_spec(meta.shape),
            _const_spec(w_in.shape),
            _const_spec(conv_w.shape),
            _const_spec(w_out.shape),
            _const_spec(g.shape),
            _const_spec(b.shape),
        ],
        out_specs=pl.BlockSpec((None, tq, d), lambda i, j: (i, j, 0)),
        out_shape=jax.ShapeDtypeStruct((bsz, t, d), F32),
        scratch_shapes=[pltpu.VMEM((tq + CONV_HALO, d), F32)],
        compiler_params=_cparams(2),
        name="conv_ln",
    )(x3, meta, w_in, conv_w, w_out, g, b)


def _split_hi_lo(v):
    hi = v.astype(BF16)
    lo = (v - hi.astype(F32)).astype(BF16)
    return hi, lo


def _dsa_proj_kernel(x_ref, wq_ref, wukt_ref, wckv_ref, wqi3t_ref, wk3_ref, wwt_ref,
                     qlatT_ref, ckv_ref, ckvT_ref, qiT_ref, kcat_ref, wT_ref):
    xb = x_ref[...].astype(BF16)
    hd = wukt_ref.shape[2]
    q = _dot(xb, wq_ref[...])
    for h in range(N_HEADS):
        qh = q[:, h * hd:(h + 1) * hd].astype(BF16)
        qlatT_ref[h] = _dot_nt(wukt_ref[h], qh).astype(BF16)
    ckv = _dot(xb, wckv_ref[...])
    ckv_ref[...] = ckv.astype(BF16)
    ckvT_ref[...] = ckv.T.astype(BF16)

    qi3 = _dot_nt(wqi3t_ref[...], xb)
    hi, lo = _split_hi_lo(qi3)
    row = lax.broadcasted_iota(I32, qi3.shape, 0) % (3 * IDX_DIM)
    is_lo = (row >= IDX_DIM) & (row < 2 * IDX_DIM)
    qsel = jnp.where(is_lo, lo, hi)
    for h in range(IDX_HEADS):
        qiT_ref[h] = qsel[h * 3 * IDX_DIM:(h + 1) * 3 * IDX_DIM, :]
    k3 = _dot(xb, wk3_ref[...])
    khi, klo = _split_hi_lo(k3)
    col = lax.broadcasted_iota(I32, k3.shape, 1)
    kcat_ref[...] = jnp.where(col >= 2 * IDX_DIM, klo, khi)
    wT_ref[...] = _dot_nt(wwt_ref[...], xb)


def _dsa_proj(x3, wq, wukt, wckv, wqi3t, wk3, wwt):
    bsz, t, d = x3.shape
    tt = DSA_TILE
    nt = t // tt
    lat = wckv.shape[1]
    outs = pl.pallas_call(
        _dsa_proj_kernel,
        grid=(bsz, nt),
        in_specs=[
            pl.BlockSpec((None, tt, d), lambda i, j: (i, j, 0)),
            _const_spec(wq.shape),
            _const_spec(wukt.shape),
            _const_spec(wckv.shape),
            _const_spec(wqi3t.shape),
            _const_spec(wk3.shape),
            _const_spec(wwt.shape),
        ],
        out_specs=[
            pl.BlockSpec((None, N_HEADS, lat, tt), lambda i, j: (i, 0, 0, j)),
            pl.BlockSpec((None, tt, lat), lambda i, j: (i, j, 0)),
            pl.BlockSpec((None, None, lat, tt), lambda i, j: (i, j, 0, 0)),
            pl.BlockSpec((None, IDX_HEADS, 3 * IDX_DIM, tt), lambda i, j: (i, 0, 0, j)),
            pl.BlockSpec((None, tt, 3 * IDX_DIM), lambda i, j: (i, j, 0)),
            pl.BlockSpec((None, IDX_HEADS, tt), lambda i, j: (i, 0, j)),
        ],
        out_shape=[
            jax.ShapeDtypeStruct((bsz, N_HEADS, lat, t), BF16),
            jax.ShapeDtypeStruct((bsz, t, lat), BF16),
            jax.ShapeDtypeStruct((bsz, nt, lat, tt), BF16),
            jax.ShapeDtypeStruct((bsz, IDX_HEADS, 3 * IDX_DIM, t), BF16),
            jax.ShapeDtypeStruct((bsz, t, 3 * IDX_DIM), BF16),
            jax.ShapeDtypeStruct((bsz, IDX_HEADS, t), F32),
        ],
        compiler_params=_cparams(2),
        name="dsa_proj",
    )(x3, wq, wukt, wckv, wqi3t, wk3, wwt)
    return outs


def _dsa_attn_kernel(x_ref, qiT_ref, wT_ref, qlatT_ref, kcat_ref, ckv_ref, ckvT_ref,
                     kcat_m_ref, ckv_m_ref, ckvT_m_ref, wuvt_ref, wout_ref, g_ref, b_ref,
                     o_ref, scores, scores_m, thr_s, acc_s, lg_a, lg_b, p_a, p_b, outT,
                     *, meta_only, topk, head_dim):
    tt = DSA_TILE
    j = pl.program_id(1)
    nx = 0 if meta_only else j + 1

    def tile_rows(t):
        return pl.ds(pl.multiple_of(t * tt, tt), tt)

    def raw_scores(kc):
        acc = jnp.zeros((kc.shape[0], tt), F32)
        for h in range(IDX_HEADS):
            s = _dot(kc, qiT_ref[h])
            acc = acc + jnp.maximum(s, 0.0) * wT_ref[h:h + 1, :]
        return acc

    sm = raw_scores(kcat_m_ref[...])
    scores_m[...] = sm

    def score_body(t, carry):
        mx, mn = carry
        row = lax.broadcasted_iota(I32, (tt, tt), 0)
        col = lax.broadcasted_iota(I32, (tt, tt), 1)
        kchunk = t * (tt // CHUNK) + row // CHUNK
        qchunk = j * (tt // CHUNK) + col // CHUNK
        acc = raw_scores(kcat_ref[tile_rows(t), :])
        scores[t] = jnp.where(kchunk <= qchunk, acc, NEG_INF)
        return (jnp.maximum(mx, acc.max(axis=0, keepdims=True)),
                jnp.minimum(mn, acc.min(axis=0, keepdims=True)))

    mx, mn = lax.fori_loop(0, nx, score_body, (sm.max(axis=0, keepdims=True),
                                               sm.min(axis=0, keepdims=True)))

    def chained_count(hits):
        def body(t, accs):
            accs = list(accs)
            for i in range(tt // 8):
                c = i % COUNT_CHAINS
                accs[c] = accs[c] + hits(scores[t, i * 8:(i + 1) * 8, :])
            return tuple(accs)

        zero = jnp.zeros((8, tt), hits(scores_m[0:8, :]).dtype)
        init = (hits(scores_m[0:8, :]), hits(scores_m[8:N_META, :])) + (zero,) * (COUNT_CHAINS - 2)
        accs = lax.fori_loop(0, nx, body, init)
        total = accs[0]
        for a in accs[1:]:
            total = total + a
        return total.sum(axis=0, keepdims=True)

    def count(thr, strict):
        return chained_count(lambda v: jnp.where((v > thr) if strict else (v >= thr), 1.0, 0.0))

    kf = float(topk)
    if meta_only:
        n_valid = jnp.full((1, tt), float(N_META), F32)
    else:
        qpos = j * tt + lax.broadcasted_iota(I32, (1, tt), 1)
        n_valid = (N_META + CHUNK * (qpos // CHUNK + 1)).astype(F32)
    all_sel = n_valid <= kf
    c0 = count(jnp.zeros((1, tt), F32), False)
    pos = c0 > kf
    one, zero = jnp.ones((1, tt), F32), jnp.zeros((1, tt), F32)
    done = jnp.where(all_sel | (c0 == kf), one, zero)
    tstar = jnp.where(all_sel, -FLT_MAX, zero)
    lo = jnp.where(pos, zero, mn - jnp.abs(mn) * BRACKET_REL - 2.0 * FLT_TINY)
    hi = jnp.where(pos, mx + jnp.abs(mx) * BRACKET_REL + 2.0 * FLT_TINY, zero)
    glo = jnp.where(pos, c0 - kf, n_valid - kf)
    ghi = jnp.where(pos, -kf, c0 - kf)

    def n_open(done, tie0, stuck):
        return jnp.sum(jnp.where((done + tie0 + stuck) == 0.0, 1.0, 0.0))

    def search_cond(c):
        return jnp.logical_and(c[0] < MAX_SEARCH_ITERS, c[1] > 0.0)

    def search_step(c):
        it, _, lo, hi, glo, ghi, done, tie0, stuck, last, tstar = c
        active = (done + tie0 + stuck) == 0.0
        cand = lo + (hi - lo) * (glo / (glo - ghi))
        cand = jnp.where((cand > lo) & (cand < hi), cand, 0.5 * lo + 0.5 * hi)
        zero_probe = jnp.logical_and(it == 0, lo == 0.0)
        cand = jnp.where(zero_probe, FLT_TINY, cand)
        inside = (cand > lo) & (cand < hi)
        g = count(cand, False) - kf
        ok = active & inside
        hit = ok & (g == 0.0)
        up = ok & (g > 0.0)
        dn = ok & (g < 0.0)
        tie_now = dn & zero_probe
        glo_n = jnp.where(up, g, jnp.where(dn & (last < 0.0), 0.5 * glo, glo))
        ghi_n = jnp.where(dn, g, jnp.where(up & (last > 0.0), 0.5 * ghi, ghi))
        done = jnp.where(hit, 1.0, done)
        tie0 = jnp.where(tie_now, 1.0, tie0)
        stuck = jnp.where(active & jnp.logical_not(inside), 1.0, stuck)
        return (it + 1, n_open(done, tie0, stuck), jnp.where(up, cand, lo), jnp.where(dn, cand, hi),
                glo_n, ghi_n, done, tie0, stuck,
                jnp.where(up, 1.0, jnp.where(dn, -1.0, last)), jnp.where(hit, cand, tstar))

    res = lax.while_loop(search_cond, search_step,
                         (jnp.int32(0), n_open(done, zero, zero), lo, hi, glo, ghi,
                          done, zero, zero, zero, tstar))
    done, tie0, tstar = res[6], res[7], res[10]
    resolved = (done + tie0) > 0.0
    thr_s[0:1, :] = jnp.where(tie0 > 0.0, 0.0, tstar)
    thr_s[1:2, :] = tie0

    @pl.when(jnp.sum(jnp.where(resolved, 0.0, 1.0)) > 0.0)
    def _():
        def int_key(v):
            bits = pltpu.bitcast(jnp.where(v == 0.0, 0.0, v), I32)
            return jnp.where(bits < 0, bits ^ 0x7FFFFFFF, bits)

        def bisect(i, kth):
            cand = kth + lax.shift_left(jnp.int32(1), 31 - i)
            cnt = chained_count(lambda v: jnp.where(int_key(v) >= cand, 1, 0))
            return jnp.where(cnt >= topk, cand, kth)

        kth = lax.fori_loop(0, 32, bisect, jnp.full((1, tt), INT_MIN, I32))
        t_fb = pltpu.bitcast(jnp.where(kth < 0, kth ^ 0x7FFFFFFF, kth), F32)
        ties_fb = jnp.where(count(t_fb, False) > kf, 1.0, 0.0)
        thr_s[0:1, :] = jnp.where(resolved, thr_s[0:1, :], t_fb)
        thr_s[1:2, :] = jnp.where(resolved, thr_s[1:2, :], ties_fb)

    thr = thr_s[0:1, :]
    has_ties = jnp.sum(thr_s[1:2, :]) > 0.0

    @pl.when(jnp.logical_not(has_ties))
    def _():
        scores_m[...] = jnp.where(scores_m[...] >= thr, 0.0, NEG_INF)

        def body(t, carry):
            scores[t] = jnp.where(scores[t] >= thr, 0.0, NEG_INF)
            return carry

        lax.fori_loop(0, nx, body, 0)

    @pl.when(has_ties)
    def _():
        need = kf - count(thr, True)

        def select(v, before):
            eq = v == thr
            sel = (v > thr) | (eq & (before < need))
            return jnp.where(sel, 0.0, NEG_INF)

        def strictly_lower(n):
            r = lax.broadcasted_iota(I32, (n, n), 0)
            c = lax.broadcasted_iota(I32, (n, n), 1)
            return jnp.where(c < r, 1.0, 0.0).astype(BF16)

        vm = scores_m[...]
        eqm = jnp.where(vm == thr, 1.0, 0.0)
        scores_m[...] = select(vm, _dot(strictly_lower(N_META), eqm.astype(BF16)))
        ltri = strictly_lower(tt)

        def body(t, taken):
            v = scores[t]
            eqf = jnp.where(v == thr, 1.0, 0.0)
            scores[t] = select(v, _dot(ltri, eqf.astype(BF16)) + taken)
            return taken + eqf.sum(axis=0, keepdims=True)

        lax.fori_loop(0, nx, body, eqm.sum(axis=0, keepdims=True))

    c_exp = (head_dim ** -0.5) * 1.4426950408889634

    bias_m = scores_m[...]
    ms, ls = [], []
    for h in range(N_HEADS):
        lg = _dot(ckv_m_ref[...], qlatT_ref[h]) + bias_m
        m = jnp.maximum(lg.max(axis=0, keepdims=True), MAX_FLOOR)
        p = jnp.exp2((lg - m) * c_exp)
        ms.append(m)
        ls.append(p.sum(axis=0, keepdims=True))
        acc_s[h] = _dot(ckvT_m_ref[...], p.astype(BF16))

    def tile_step(t, carry):
        m_all, l_all = carry
        kv = ckv_ref[tile_rows(t), :]
        kvT = ckvT_ref[t]
        ms, ls = [], []

        def tree(vals, op):
            while len(vals) > 1:
                vals = [op(a, b) for a, b in zip(vals[0::2], vals[1::2])]
            return vals[0]

        par = t & 1

        def biased_max(lg):
            accs = [None] * SOFTMAX_CHAINS
            for i in range(tt // 8):
                rows = slice(i * 8, (i + 1) * 8)
                x = lg[par, rows, :] + scores[t, rows, :]
                lg[par, rows, :] = x
                c = i % SOFTMAX_CHAINS
                accs[c] = x if accs[c] is None else jnp.maximum(accs[c], x)
            return tree(accs, jnp.maximum).max(axis=0, keepdims=True)

        def exp_chunks(lg, ps, m_new):
            accs = [None] * SOFTMAX_CHAINS
            for i in range(tt // I16_ROWS):
                rows = slice(i * I16_ROWS, (i + 1) * I16_ROWS)
                p = jnp.exp2((lg[par, rows, :] - m_new) * c_exp)
                ps[par, rows, :] = p.astype(BF16)
                c = i % SOFTMAX_CHAINS
                accs[c] = p if accs[c] is None else accs[c] + p
            return tree(accs, jnp.add).sum(axis=0, keepdims=True)

        lgs, pss = (lg_a, lg_b), (p_a, p_b)
        lgs[0][par] = _dot(kv, qlatT_ref[0])
        for h in range(N_HEADS):
            if h + 1 < N_HEADS:
                lgs[(h + 1) % 2][par] = _dot(kv, qlatT_ref[h + 1])
            lg, ps = lgs[h % 2], pss[h % 2]
            m_old = m_all[h:h + 1, :]
            m_new = jnp.maximum(m_old, biased_max(lg))
            a = jnp.exp2((m_old - m_new) * c_exp)
            l_tile = exp_chunks(lg, ps, m_new)
            ms.append(m_new)
            ls.append(a * l_all[h:h + 1, :] + l_tile)
            acc_s[h] = a * acc_s[h] + _dot(kvT, ps[par])
        return jnp.concatenate(ms, axis=0), jnp.concatenate(ls, axis=0)

    _, l_all = lax.fori_loop(0, nx, tile_step,
                             (jnp.concatenate(ms, axis=0), jnp.concatenate(ls, axis=0)))

    hd = wuvt_ref.shape[1]
    inv_l = 1.0 / l_all
    for h in range(N_HEADS):
        ctx = (acc_s[h] * inv_l[h:h + 1, :]).astype(BF16)
        outT[h * hd:(h + 1) * hd, :] = _dot(wuvt_ref[h], ctx)

    out = outT[...].T.astype(BF16)
    mix = _dot(out, wout_ref[...])
    o_ref[...] = _layer_norm(ALPHA * x_ref[...] + mix, g_ref[...], b_ref[...])


def _dsa_attn(x3, proj, proj_meta, wuvt, w_out, g, b, *, meta_only, topk):
    bsz, t, d = x3.shape
    tt = DSA_TILE
    nt = t // tt
    qlatT, ckv, ckvT, qiT, kcat, wT = proj
    _, ckv_m, ckvT_m, _, kcat_m, _ = proj_meta
    lat = ckv.shape[-1]
    kcat_m, ckv_m, ckvT_m = kcat_m[0, :N_META], ckv_m[0, :N_META], ckvT_m[0, 0, :, :N_META]
    head_dim = wuvt.shape[1]
    return pl.pallas_call(
        functools.partial(_dsa_attn_kernel, meta_only=meta_only, topk=topk, head_dim=head_dim),
        grid=(bsz, nt),
        in_specs=[
            pl.BlockSpec((None, tt, d), lambda i, j: (i, j, 0)),
            pl.BlockSpec((None, IDX_HEADS, 3 * IDX_DIM, tt), lambda i, j: (i, 0, 0, j)),
            pl.BlockSpec((None, IDX_HEADS, tt), lambda i, j: (i, 0, j)),
            pl.BlockSpec((None, N_HEADS, lat, tt), lambda i, j: (i, 0, 0, j)),
            pl.BlockSpec((None, t, 3 * IDX_DIM), lambda i, j: (i, 0, 0),
                         pipeline_mode=pl.Buffered(1)),
            pl.BlockSpec((None, t, lat), lambda i, j: (i, 0, 0),
                         pipeline_mode=pl.Buffered(1)),
            pl.BlockSpec((None, nt, lat, tt), lambda i, j: (i, 0, 0, 0),
                         pipeline_mode=pl.Buffered(1)),
            _const_spec(kcat_m.shape),
            _const_spec(ckv_m.shape),
            _const_spec(ckvT_m.shape),
            _const_spec(wuvt.shape),
            _const_spec(w_out.shape),
            _const_spec(g.shape),
            _const_spec(b.shape),
        ],
        out_specs=pl.BlockSpec((None, tt, d), lambda i, j: (i, j, 0)),
        out_shape=jax.ShapeDtypeStruct((bsz, t, d), F32),
        scratch_shapes=[
            pltpu.VMEM((nt, tt, tt), F32),
            pltpu.VMEM((N_META, tt), F32),
            pltpu.VMEM((8, tt), F32),
            pltpu.VMEM((N_HEADS, lat, tt), F32),
            pltpu.VMEM((2, tt, tt), F32),
            pltpu.VMEM((2, tt, tt), F32),
            pltpu.VMEM((2, tt, tt), BF16),
            pltpu.VMEM((2, tt, tt), BF16),
            pltpu.VMEM((d, tt), F32),
        ],
        compiler_params=_cparams(2),
        name="dsa_attn_ln",
    )(x3, qiT, wT, qlatT, kcat, ckv, ckvT, kcat_m, ckv_m, ckvT_m, wuvt, w_out, g, b)


def _row(v):
    return v.reshape(1, -1)


def kernel(x, meta, a_w_in, a_w_group, a_scale, a_w_out, b_w_in, b_w_uk, b_w_uv, b_w_out,
           c_w_in, c_conv, c_w_out, ln_mix_g, ln_mix_b, ffn_w_gu, ffn_w_down, ln_ffn_g, ln_ffn_b):
    bsz, seq, d = x.shape
    assert seq % DSA_TILE == 0 and meta.shape[0] == N_META
    topk = min(TOPK_MAX, seq // 4)
    lat = d // KV_LATENT_DIV
    row_tile = 2 * FFN_SUB_ROWS if (bsz * seq) % (2 * FFN_SUB_ROWS) == 0 else FFN_SUB_ROWS
    mixer_tile = 2 * MIXER_SUB_ROWS if seq % (2 * MIXER_SUB_ROWS) == 0 else MIXER_SUB_ROWS

    hx = x
    hm = meta.astype(x.dtype)

    for i in range(DEPTH):
        kind, jj = i % 3, i // 3
        g, b = _row(ln_mix_g[i]), _row(ln_mix_b[i])
        if kind == 0:
            args = (a_w_in[jj].astype(BF16), a_w_group[jj].astype(BF16), _row(a_scale[jj]),
                    a_w_out[jj].astype(BF16), g, b)
            hx_new = _pool_layer(hx, hm, *args, tq=mixer_tile, is_meta=False)
            hm = _pool_layer(hm[None], hm, *args, tq=N_META, is_meta=True)[0]
            hx = hx_new
        elif kind == 1:
            w = b_w_in[jj]
            o = 0
            wq = w[:, o:o + d].astype(BF16); o += d
            wckv = w[:, o:o + lat].astype(BF16); o += lat
            wqi = w[:, o:o + IDX_HEADS * IDX_DIM]; o += IDX_HEADS * IDX_DIM
            wk = w[:, o:o + IDX_DIM]; o += IDX_DIM
            ww = w[:, o:o + IDX_HEADS]
            wqi3t = jnp.tile(wqi.T.reshape(IDX_HEADS, 1, IDX_DIM, d), (1, 3, 1, 1))
            wqi3t = wqi3t.reshape(IDX_HEADS * 3 * IDX_DIM, d).astype(BF16)
            wk3 = jnp.tile(wk, (1, 3)).astype(BF16)
            wwt = ww.T.astype(BF16)
            wukt = jnp.swapaxes(b_w_uk[jj], 1, 2).astype(BF16)
            wuvt = jnp.swapaxes(b_w_uv[jj], 1, 2).astype(BF16)
            wout = b_w_out[jj].astype(BF16)
            pw = (wq, wukt, wckv, wqi3t, wk3, wwt)
            hm_pad = jnp.pad(hm, ((0, DSA_TILE - N_META), (0, 0)))[None]
            proj_m = _dsa_proj(hm_pad, *pw)
            proj_x = _dsa_proj(hx, *pw)
            hx_new = _dsa_attn(hx, proj_x, proj_m, wuvt, wout, g, b, meta_only=False, topk=topk)
            hm = _dsa_attn(hm_pad, proj_m, proj_m, wuvt, wout, g, b, meta_only=True,
                           topk=topk)[0, :N_META]
            hx = hx_new
        else:
            args = (c_w_in[jj].astype(BF16), c_conv[jj], c_w_out[jj].astype(BF16), g, b)
            hx_new = _conv_layer(hx, hm, *args, tq=mixer_tile, is_meta=False)
            hm = _conv_layer(hm[None], hm, *args, tq=N_META, is_meta=True)[0]
            hx = hx_new

        fargs = (ffn_w_gu[i].astype(BF16), ffn_w_down[i].astype(BF16),
                 _row(ln_ffn_g[i]), _row(ln_ffn_b[i]))
        hx = _ffn(hx.reshape(bsz * seq, d), *fargs, tile=row_tile).reshape(bsz, seq, d)
        if i + 1 < DEPTH:
            hm = _ffn(hm, *fargs, tile=N_META)
    return hx
```

```python
import functools

import jax
import jax.numpy as jnp
from jax import lax
from jax.experimental import pallas as pl
from jax.experimental.pallas import tpu as pltpu

CHUNK = 64
N_META = 16
POOL_WINDOWS = (2, 4, 8, 16)
N_HEADS = 8
KV_LATENT_DIV = 4
IDX_HEADS = 8
IDX_DIM = 64
TOPK_MAX = 256
CONV_WIDTH = 3
DEPTH = 4
ALPHA = (2.0 * DEPTH) ** 0.25
LN_EPS = 1e-5

BF16 = jnp.bfloat16
F32 = jnp.float32
I32 = jnp.int32

VMEM_LIMIT_BYTES = 56 * 1024 * 1024
DSA_TILE = 512
I16_ROWS = 16
INT_MIN = -(2 ** 31)
FLT_MAX = 3.4028234663852886e38
FLT_TINY = 1.1754943508222875e-38
BRACKET_REL = 1e-6
MAX_SEARCH_ITERS = 40
PEEL_STEPS = 2
COUNT_CHAINS = 4
SOFTMAX_CHAINS = 4
NEG_INF = float("-inf")
MAX_FLOOR = -1e30


def _cparams(n_axes):
    return pltpu.CompilerParams(
        dimension_semantics=("arbitrary",) * n_axes,
        vmem_limit_bytes=VMEM_LIMIT_BYTES,
    )


def _const_spec(shape):
    nd = len(shape)
    return pl.BlockSpec(shape, lambda *_: (0,) * nd, pipeline_mode=pl.Buffered(1))


def _layer_norm(z, g, b):
    mu = jnp.mean(z, axis=-1, keepdims=True)
    zc = z - mu
    var = jnp.mean(zc * zc, axis=-1, keepdims=True)
    return zc * lax.rsqrt(var + LN_EPS) * g + b


def _dot(a, b):
    return jnp.dot(a, b, preferred_element_type=F32)


def _dot_nt(a, b):
    return lax.dot_general(a, b, (((1,), (1,)), ((), ())), preferred_element_type=F32)


FFN_SUB_ROWS = 512

def _ffn_kernel(h_ref, wgu_ref, wd_ref, g_ref, b_ref, o_ref, *, d_ff, ff_chunk):
    rows = h_ref.shape[0]
    sub = min(rows, FFN_SUB_ROWS)
    for r0 in range(0, rows, sub):
        h = h_ref[r0:r0 + sub, :]
        hb = h.astype(BF16)
        acc = ALPHA * h
        for c in range(d_ff // ff_chunk):
            lo = c * ff_chunk
            gate = _dot(hb, wgu_ref[:, lo:lo + ff_chunk])
            up = _dot(hb, wgu_ref[:, d_ff + lo:d_ff + lo + ff_chunk])
            act = (gate * jax.nn.sigmoid(gate) * up).astype(BF16)
            acc = acc + _dot(act, wd_ref[lo:lo + ff_chunk, :])
        o_ref[r0:r0 + sub, :] = _layer_norm(acc, g_ref[...], b_ref[...])


def _ffn(h2d, w_gu, w_down, g, b, *, tile):
    n, d = h2d.shape
    d_ff = w_down.shape[0]
    ff_chunk = 256 if d_ff % 256 == 0 else d_ff
    return pl.pallas_call(
        functools.partial(_ffn_kernel, d_ff=d_ff, ff_chunk=ff_chunk),
        grid=(n // tile,),
        in_specs=[
            pl.BlockSpec((tile, d), lambda i: (i, 0)),
            _const_spec(w_gu.shape),
            _const_spec(w_down.shape),
            _const_spec(g.shape),
            _const_spec(b.shape),
        ],
        out_specs=pl.BlockSpec((tile, d), lambda i: (i, 0)),
        out_shape=jax.ShapeDtypeStruct((n, d), F32),
        compiler_params=_cparams(1),
        name="ffn_ln",
    )(h2d, w_gu, w_down, g, b)


POOL_HALO = 16
POOL_PAD = 8
MIXER_SUB_ROWS = 512


def _pool_kernel(x_ref, meta_ref, win_ref, wg_ref, sc_ref, wout_ref, g_ref, b_ref,
                 o_ref, ubuf, s2, s4, s8, *, tq, is_meta):
    j = pl.program_id(1)
    d = x_ref.shape[-1]
    gw = d // len(POOL_WINDOWS)
    sub = min(tq, MIXER_SUB_ROWS)
    pad, top = POOL_PAD, POOL_PAD + POOL_HALO

    ubuf[0:pad, :] = jnp.zeros((pad, d), F32)
    if is_meta:
        ubuf[pad:top, :] = jnp.zeros((POOL_HALO, d), F32)
    else:
        @pl.when(j == 0)
        def _():
            ubuf[pad:top, :] = _dot(meta_ref[...].astype(BF16), win_ref[...])

        @pl.when(j > 0)
        def _():
            ubuf[pad:top, :] = ubuf[pad + tq:top + tq, :]

    for si, r0 in enumerate(range(0, tq, sub)):
        x = x_ref[r0:r0 + sub, :]
        u = _dot(x.astype(BF16), win_ref[...])
        base = top + r0
        ubuf[base:base + sub, :] = u

        ext = sub + POOL_HALO
        e0 = base - POOL_HALO
        s2[si, 0:pad, :] = jnp.zeros((pad, d), F32)
        s2[si, pad:pad + ext, :] = ubuf[e0:e0 + ext, :] + ubuf[e0 - 1:e0 - 1 + ext, :]
        s4[si, 0:pad, :] = jnp.zeros((pad, d - gw), F32)
        s4[si, pad:pad + ext, :] = s2[si, pad:pad + ext, gw:] + s2[si, pad - 2:pad - 2 + ext, gw:]
        s8[si, :, :] = s4[si, pad:pad + ext, gw:] + s4[si, pad - 4:pad - 4 + ext, gw:]
        t0 = pad + POOL_HALO
        win_sums = (
            s2[si, t0:t0 + sub, 0:gw],
            s4[si, t0:t0 + sub, 0:gw],
            s8[si, POOL_HALO:POOL_HALO + sub, 0:gw],
            s8[si, POOL_HALO:POOL_HALO + sub, gw:] + s8[si, POOL_HALO - 8:POOL_HALO - 8 + sub, gw:],
        )

        ys = []
        for gi, w in enumerate(POOL_WINDOWS):
            c0 = gi * gw
            s = win_sums[gi]
            if is_meta:
                pos = lax.broadcasted_iota(I32, (sub, 1), 0) + 1
                cnt = jnp.minimum(pos, w).astype(F32)
                pooled = s / cnt
            else:
                pooled = s / float(w)
            dlt = (pooled - u[:, c0:c0 + gw]).astype(BF16)
            ys.append(_dot(dlt, wg_ref[gi]))
        y = jnp.concatenate(ys, axis=-1) * sc_ref[...]
        m = _dot(y.astype(BF16), wout_ref[...])
        o_ref[r0:r0 + sub, :] = _layer_norm(ALPHA * x + m, g_ref[...], b_ref[...])


def _pool_layer(x3, meta, w_in, w_group, scale, w_out, g, b, *, tq, is_meta):
    bsz, t, d = x3.shape
    assert POOL_WINDOWS == (2, 4, 8, 16)
    gw = d // len(POOL_WINDOWS)
    sub = min(tq, MIXER_SUB_ROWS)
    n_sub, ext = tq // sub, sub + POOL_HALO
    return pl.pallas_call(
        functools.partial(_pool_kernel, tq=tq, is_meta=is_meta),
        grid=(bsz, t // tq),
        in_specs=[
            pl.BlockSpec((None, tq, d), lambda i, j: (i, j, 0)),
            _const_spec(meta.shape),
            _const_spec(w_in.shape),
            _const_spec(w_group.shape),
            _const_spec(scale.shape),
            _const_spec(w_out.shape),
            _const_spec(g.shape),
            _const_spec(b.shape),
        ],
        out_specs=pl.BlockSpec((None, tq, d), lambda i, j: (i, j, 0)),
        out_shape=jax.ShapeDtypeStruct((bsz, t, d), F32),
        scratch_shapes=[
            pltpu.VMEM((POOL_PAD + POOL_HALO + tq, d), F32),
            pltpu.VMEM((n_sub, POOL_PAD + ext, d), F32),
            pltpu.VMEM((n_sub, POOL_PAD + ext, d - gw), F32),
            pltpu.VMEM((n_sub, ext, d - 2 * gw), F32),
        ],
        compiler_params=_cparams(2),
        name="pool_ln",
    )(x3, meta, w_in, w_group, scale, w_out, g, b)


CONV_HALO = 8


def _conv_kernel(x_ref, meta_ref, win_ref, cw_ref, wout_ref, g_ref, b_ref,
                 o_ref, zbuf, *, tq, is_meta):
    j = pl.program_id(1)
    d = x_ref.shape[-1]
    sub = min(tq, MIXER_SUB_ROWS)

    def gates(vb):
        bg = _dot(vb, win_ref[:, 0:d])
        cg = _dot(vb, win_ref[:, d:2 * d])
        hv = _dot(vb, win_ref[:, 2 * d:3 * d])
        return bg, cg * hv

    if is_meta:
        zbuf[0:CONV_HALO, :] = jnp.zeros((CONV_HALO, d), F32)
    else:
        @pl.when(j == 0)
        def _():
            _, zm = gates(meta_ref[...].astype(BF16))
            zbuf[0:CONV_HALO, :] = zm[N_META - CONV_HALO:N_META, :]

        @pl.when(j > 0)
        def _():
            zbuf[0:CONV_HALO, :] = zbuf[tq:tq + CONV_HALO, :]

    for r0 in range(0, tq, sub):
        x = x_ref[r0:r0 + sub, :]
        bg, z = gates(x.astype(BF16))
        base = CONV_HALO + r0
        zbuf[base:base + sub, :] = z
        conv = z * cw_ref[CONV_WIDTH - 1:CONV_WIDTH, :]
        for k in range(CONV_WIDTH - 1):
            sh = CONV_WIDTH - 1 - k
            conv = conv + zbuf[base - sh:base - sh + sub, :] * cw_ref[k:k + 1, :]
        m = _dot((bg * conv).astype(BF16), wout_ref[...])
        o_ref[r0:r0 + sub, :] = _layer_norm(ALPHA * x + m, g_ref[...], b_ref[...])


def _conv_layer(x3, meta, w_in, conv_w, w_out, g, b, *, tq, is_meta):
    bsz, t, d = x3.shape
    return pl.pallas_call(
        functools.partial(_conv_kernel, tq=tq, is_meta=is_meta),
        grid=(bsz, t // tq),
        in_specs=[
            pl.BlockSpec((None, tq, d), lambda i, j: (i, j, 0)),
            _const_spec(meta.shape),
            _const_spec(w_in.shape),
            _const_spec(conv_w.shape),
            _const_spec(w_out.shape),
            _const_spec(g.shape),
            _const_spec(b.shape),
        ],
        out_specs=pl.BlockSpec((None, tq, d), lambda i, j: (i, j, 0)),
        out_shape=jax.ShapeDtypeStruct((bsz, t, d), F32),
        scratch_shapes=[pltpu.VMEM((tq + CONV_HALO, d), F32)],
        compiler_params=_cparams(2),
        name="conv_ln",
    )(x3, meta, w_in, conv_w, w_out, g, b)


def _split_hi_lo(v):
    hi = v.astype(BF16)
    lo = (v - hi.astype(F32)).astype(BF16)
    return hi, lo


def _dsa_proj_kernel(x_ref, wq_ref, wukt_ref, wckv_ref, wqi3t_ref, wk3_ref, wwt_ref,
                     qlatT_ref, ckv_ref, ckvT_ref, qiT_ref, kcat_ref, wT_ref):
    xb = x_ref[...].astype(BF16)
    hd = wukt_ref.shape[2]
    q = _dot(xb, wq_ref[...])
    for h in range(N_HEADS):
        qh = q[:, h * hd:(h + 1) * hd].astype(BF16)
        qlatT_ref[h] = _dot_nt(wukt_ref[h], qh).astype(BF16)
    ckv = _dot(xb, wckv_ref[...])
    ckv_ref[...] = ckv.astype(BF16)
    ckvT_ref[...] = ckv.T.astype(BF16)

    qi3 = _dot_nt(wqi3t_ref[...], xb)
    hi, lo = _split_hi_lo(qi3)
    row = lax.broadcasted_iota(I32, qi3.shape, 0) % (3 * IDX_DIM)
    is_lo = (row >= IDX_DIM) & (row < 2 * IDX_DIM)
    qsel = jnp.where(is_lo, lo, hi)
    for h in range(IDX_HEADS):
        qiT_ref[h] = qsel[h * 3 * IDX_DIM:(h + 1) * 3 * IDX_DIM, :]
    k3 = _dot(xb, wk3_ref[...])
    khi, klo = _split_hi_lo(k3)
    col = lax.broadcasted_iota(I32, k3.shape, 1)
    kcat_ref[...] = jnp.where(col >= 2 * IDX_DIM, klo, khi)
    wT_ref[...] = _dot_nt(wwt_ref[...], xb)


def _dsa_proj(x3, wq, wukt, wckv, wqi3t, wk3, wwt):
    bsz, t, d = x3.shape
    tt = DSA_TILE
    nt = t // tt
    lat = wckv.shape[1]
    outs = pl.pallas_call(
        _dsa_proj_kernel,
        grid=(bsz, nt),
        in_specs=[
            pl.BlockSpec((None, tt, d), lambda i, j: (i, j, 0)),
            _const_spec(wq.shape),
            _const_spec(wukt.shape),
            _const_spec(wckv.shape),
            _const_spec(wqi3t.shape),
            _const_spec(wk3.shape),
            _const_spec(wwt.shape),
        ],
        out_specs=[
            pl.BlockSpec((None, N_HEADS, lat, tt), lambda i, j: (i, 0, 0, j)),
            pl.BlockSpec((None, tt, lat), lambda i, j: (i, j, 0)),
            pl.BlockSpec((None, None, lat, tt), lambda i, j: (i, j, 0, 0)),
            pl.BlockSpec((None, IDX_HEADS, 3 * IDX_DIM, tt), lambda i, j: (i, 0, 0, j)),
            pl.BlockSpec((None, tt, 3 * IDX_DIM), lambda i, j: (i, j, 0)),
            pl.BlockSpec((None, IDX_HEADS, tt), lambda i, j: (i, 0, j)),
        ],
        out_shape=[
            jax.ShapeDtypeStruct((bsz, N_HEADS, lat, t), BF16),
            jax.ShapeDtypeStruct((bsz, t, lat), BF16),
            jax.ShapeDtypeStruct((bsz, nt, lat, tt), BF16),
            jax.ShapeDtypeStruct((bsz, IDX_HEADS, 3 * IDX_DIM, t), BF16),
            jax.ShapeDtypeStruct((bsz, t, 3 * IDX_DIM), BF16),
            jax.ShapeDtypeStruct((bsz, IDX_HEADS, t), F32),
        ],
        compiler_params=_cparams(2),
        name="dsa_proj",
    )(x3, wq, wukt, wckv, wqi3t, wk3, wwt)
    return outs


def _dsa_attn_kernel(x_ref, qiT_ref, wT_ref, qlatT_ref, kcat_ref, ckv_ref, ckvT_ref,
                     kcat_m_ref, ckv_m_ref, ckvT_m_ref, wuvt_ref, wout_ref, g_ref, b_ref,
                     o_ref, scores, scores_m, thr_s, acc_s, lg_a, lg_b, p_a, p_b, outT,
                     *, meta_only, topk, head_dim):
    tt = DSA_TILE
    j = pl.program_id(1)
    nx = 0 if meta_only else j + 1

    def tile_rows(t):
        return pl.ds(pl.multiple_of(t * tt, tt), tt)

    def raw_scores(kc):
        acc = jnp.zeros((kc.shape[0], tt), F32)
        for h in range(IDX_HEADS):
            s = _dot(kc, qiT_ref[h])
            acc = acc + jnp.maximum(s, 0.0) * wT_ref[h:h + 1, :]
        return acc

    sm = raw_scores(kcat_m_ref[...])
    scores_m[...] = sm

    def score_body(t, carry):
        mx, mn = carry
        row = lax.broadcasted_iota(I32, (tt, tt), 0)
        col = lax.broadcasted_iota(I32, (tt, tt), 1)
        kchunk = t * (tt // CHUNK) + row // CHUNK
        qchunk = j * (tt // CHUNK) + col // CHUNK
        acc = raw_scores(kcat_ref[tile_rows(t), :])
        scores[t] = jnp.where(kchunk <= qchunk, acc, NEG_INF)
        return (jnp.maximum(mx, acc.max(axis=0, keepdims=True)),
                jnp.minimum(mn, acc.min(axis=0, keepdims=True)))

    mx, mn = lax.fori_loop(0, nx, score_body, (sm.max(axis=0, keepdims=True),
                                               sm.min(axis=0, keepdims=True)))

    def chained_reduce(part, combine, neutral):
        def body(t, accs):
            accs = list(accs)
            for i in range(tt // 8):
                c = i % COUNT_CHAINS
                accs[c] = combine(accs[c], part(scores[t, i * 8:(i + 1) * 8, :]))
            return tuple(accs)

        first = part(scores_m[0:8, :])
        fill = jnp.full((8, tt), neutral, first.dtype)
        accs = lax.fori_loop(0, nx, body,
                             (first, part(scores_m[8:N_META, :])) + (fill,) * (COUNT_CHAINS - 2))
        total = accs[0]
        for a in accs[1:]:
            total = combine(total, a)
        return total

    def chained_count(hits):
        return chained_reduce(hits, jnp.add, 0).sum(axis=0, keepdims=True)

    def count(thr, strict):
        return chained_count(lambda v: jnp.where((v > thr) if strict else (v >= thr), 1.0, 0.0))

    def max_below(bound):
        part = lambda v: jnp.where(v < bound, v, NEG_INF)
        return chained_reduce(part, jnp.maximum, NEG_INF).max(axis=0, keepdims=True)

    kf = float(topk)
    if meta_only:
        n_valid = jnp.full((1, tt), float(N_META), F32)
    else:
        qpos = j * tt + lax.broadcasted_iota(I32, (1, tt), 1)
        n_valid = (N_META + CHUNK * (qpos // CHUNK + 1)).astype(F32)
    all_sel = n_valid <= kf
    c0 = count(jnp.zeros((1, tt), F32), False)
    pos = c0 > kf
    one, zero = jnp.ones((1, tt), F32), jnp.zeros((1, tt), F32)
    done = jnp.where(all_sel | (c0 == kf), one, zero)
    tstar = jnp.where(all_sel, -FLT_MAX, zero)
    lo = jnp.where(pos, zero, mn - jnp.abs(mn) * BRACKET_REL - 2.0 * FLT_TINY)
    hi = jnp.where(pos, mx + jnp.abs(mx) * BRACKET_REL + 2.0 * FLT_TINY, zero)
    glo = jnp.where(pos, c0 - kf, n_valid - kf)
    ghi = jnp.where(pos, -kf, c0 - kf)

    def n_open(done, tie0, stuck, fhi):
        return jnp.sum(jnp.where(((done + tie0 + stuck) == 0.0) & (fhi < -float(PEEL_STEPS)), 1.0, 0.0))

    def search_cond(c):
        return jnp.logical_and(c[0] < MAX_SEARCH_ITERS, c[1] > 0.0)

    def search_step(c):
        it, _, lo, hi, glo, ghi, fhi, done, tie0, stuck, last, tstar = c
        active = (done + tie0 + stuck) == 0.0
        cand = lo + (hi - lo) * (glo / (glo - ghi))
        cand = jnp.where((cand > lo) & (cand < hi), cand, 0.5 * lo + 0.5 * hi)
        zero_probe = jnp.logical_and(it == 0, lo == 0.0)
        cand = jnp.where(zero_probe, FLT_TINY, cand)
        inside = (cand > lo) & (cand < hi)
        g = count(cand, False) - kf
        ok = active & inside
        hit = ok & (g == 0.0)
        up = ok & (g > 0.0)
        dn = ok & (g < 0.0)
        tie_now = dn & zero_probe
        glo_n = jnp.where(up, g, jnp.where(dn & (last < 0.0), 0.5 * glo, glo))
        ghi_n = jnp.where(dn, g, jnp.where(up & (last > 0.0), 0.5 * ghi, ghi))
        fhi = jnp.where(dn, g, fhi)
        done = jnp.where(hit, 1.0, done)
        tie0 = jnp.where(tie_now, 1.0, tie0)
        stuck = jnp.where(active & jnp.logical_not(inside), 1.0, stuck)
        return (it + 1, n_open(done, tie0, stuck, fhi), jnp.where(up, cand, lo),
                jnp.where(dn, cand, hi), glo_n, ghi_n, fhi, done, tie0, stuck,
                jnp.where(up, 1.0, jnp.where(dn, -1.0, last)), jnp.where(hit, cand, tstar))

    res = lax.while_loop(search_cond, search_step,
                         (jnp.int32(0), n_open(done, zero, zero, ghi), lo, hi, glo, ghi, ghi,
                          done, zero, zero, zero, tstar))
    hi, fhi, done, tie0, stuck, tstar = res[3], res[6], res[7], res[8], res[9], res[11]
    thr_s[0:1, :] = jnp.where(tie0 > 0.0, 0.0, tstar)
    thr_s[1:2, :] = tie0
    thr_s[2:3, :] = done + tie0

    peel = ((done + tie0 + stuck) == 0.0) & (fhi >= -float(PEEL_STEPS))

    @pl.when(jnp.sum(jnp.where(peel, 1.0, 0.0)) > 0.0)
    def _():
        t_peel = hi
        for i in range(PEEL_STEPS):
            t_peel = jnp.where(-fhi > float(i), max_below(t_peel), t_peel)
        good = peel & (count(t_peel, False) == kf)
        thr_s[0:1, :] = jnp.where(good, t_peel, thr_s[0:1, :])
        thr_s[2:3, :] = jnp.where(good, 1.0, thr_s[2:3, :])

    resolved = thr_s[2:3, :] > 0.0

    @pl.when(jnp.sum(jnp.where(resolved, 0.0, 1.0)) > 0.0)
    def _():
        def int_key(v):
            bits = pltpu.bitcast(jnp.where(v == 0.0, 0.0, v), I32)
            return jnp.where(bits < 0, bits ^ 0x7FFFFFFF, bits)

        def bisect(i, kth):
            cand = kth + lax.shift_left(jnp.int32(1), 31 - i)
            cnt = chained_count(lambda v: jnp.where(int_key(v) >= cand, 1, 0))
            return jnp.where(cnt >= topk, cand, kth)

        kth = lax.fori_loop(0, 32, bisect, jnp.full((1, tt), INT_MIN, I32))
        t_fb = pltpu.bitcast(jnp.where(kth < 0, kth ^ 0x7FFFFFFF, kth), F32)
        ties_fb = jnp.where(count(t_fb, False) > kf, 1.0, 0.0)
        thr_s[0:1, :] = jnp.where(resolved, thr_s[0:1, :], t_fb)
        thr_s[1:2, :] = jnp.where(resolved, thr_s[1:2, :], ties_fb)

    thr = thr_s[0:1, :]
    has_ties = jnp.sum(thr_s[1:2, :]) > 0.0

    @pl.when(jnp.logical_not(has_ties))
    def _():
        scores_m[...] = jnp.where(scores_m[...] >= thr, 0.0, NEG_INF)

        def body(t, carry):
            scores[t] = jnp.where(scores[t] >= thr, 0.0, NEG_INF)
            return carry

        lax.fori_loop(0, nx, body, 0)

    @pl.when(has_ties)
    def _():
        need = kf - count(thr, True)

        def select(v, before):
            eq = v == thr
            sel = (v > thr) | (eq & (before < need))
            return jnp.where(sel, 0.0, NEG_INF)

        def strictly_lower(n):
            r = lax.broadcasted_iota(I32, (n, n), 0)
            c = lax.broadcasted_iota(I32, (n, n), 1)
            return jnp.where(c < r, 1.0, 0.0).astype(BF16)

        vm = scores_m[...]
        eqm = jnp.where(vm == thr, 1.0, 0.0)
        scores_m[...] = select(vm, _dot(strictly_lower(N_META), eqm.astype(BF16)))
        ltri = strictly_lower(tt)

        def body(t, taken):
            v = scores[t]
            eqf = jnp.where(v == thr, 1.0, 0.0)
            scores[t] = select(v, _dot(ltri, eqf.astype(BF16)) + taken)
            return taken + eqf.sum(axis=0, keepdims=True)

        lax.fori_loop(0, nx, body, eqm.sum(axis=0, keepdims=True))

    c_exp = (head_dim ** -0.5) * 1.4426950408889634

    bias_m = scores_m[...]
    ms, ls = [], []
    for h in range(N_HEADS):
        lg = _dot(ckv_m_ref[...], qlatT_ref[h]) + bias_m
        m = jnp.maximum(lg.max(axis=0, keepdims=True), MAX_FLOOR)
        p = jnp.exp2((lg - m) * c_exp)
        ms.append(m)
        ls.append(p.sum(axis=0, keepdims=True))
        acc_s[h] = _dot(ckvT_m_ref[...], p.astype(BF16))

    def tile_step(t, carry):
        m_all, l_all = carry
        kv = ckv_ref[tile_rows(t), :]
        kvT = ckvT_ref[t]
        ms, ls = [], []

        def tree(vals, op):
            while len(vals) > 1:
                vals = [op(a, b) for a, b in zip(vals[0::2], vals[1::2])]
            return vals[0]

        par = t & 1

        def biased_max(lg):
            accs = [None] * SOFTMAX_CHAINS
            for i in range(tt // 8):
                rows = slice(i * 8, (i + 1) * 8)
                x = lg[par, rows, :] + scores[t, rows, :]
                lg[par, rows, :] = x
                c = i % SOFTMAX_CHAINS
                accs[c] = x if accs[c] is None else jnp.maximum(accs[c], x)
            return tree(accs, jnp.maximum).max(axis=0, keepdims=True)

        def exp_chunks(lg, ps, m_new):
            accs = [None] * SOFTMAX_CHAINS
            for i in range(tt // I16_ROWS):
                rows = slice(i * I16_ROWS, (i + 1) * I16_ROWS)
                p = jnp.exp2((lg[par, rows, :] - m_new) * c_exp)
                ps[par, rows, :] = p.astype(BF16)
                c = i % SOFTMAX_CHAINS
                accs[c] = p if accs[c] is None else accs[c] + p
            return tree(accs, jnp.add).sum(axis=0, keepdims=True)

        lgs, pss = (lg_a, lg_b), (p_a, p_b)
        lgs[0][par] = _dot(kv, qlatT_ref[0])
        for h in range(N_HEADS):
            if h + 1 < N_HEADS:
                lgs[(h + 1) % 2][par] = _dot(kv, qlatT_ref[h + 1])
            lg, ps = lgs[h % 2], pss[h % 2]
            m_old = m_all[h:h + 1, :]
            m_new = jnp.maximum(m_old, biased_max(lg))
            a = jnp.exp2((m_old - m_new) * c_exp)
            l_tile = exp_chunks(lg, ps, m_new)
            ms.append(m_new)
            ls.append(a * l_all[h:h + 1, :] + l_tile)
            acc_s[h] = a * acc_s[h] + _dot(kvT, ps[par])
        return jnp.concatenate(ms, axis=0), jnp.concatenate(ls, axis=0)

    _, l_all = lax.fori_loop(0, nx, tile_step,
                             (jnp.concatenate(ms, axis=0), jnp.concatenate(ls, axis=0)))

    hd = wuvt_ref.shape[1]
    inv_l = 1.0 / l_all
    for h in range(N_HEADS):
        ctx = (acc_s[h] * inv_l[h:h + 1, :]).astype(BF16)
        outT[h * hd:(h + 1) * hd, :] = _dot(wuvt_ref[h], ctx)

    out = outT[...].T.astype(BF16)
    mix = _dot(out, wout_ref[...])
    o_ref[...] = _layer_norm(ALPHA * x_ref[...] + mix, g_ref[...], b_ref[...])


def _dsa_attn(x3, proj, proj_meta, wuvt, w_out, g, b, *, meta_only, topk):
    bsz, t, d = x3.shape
    tt = DSA_TILE
    nt = t // tt
    qlatT, ckv, ckvT, qiT, kcat, wT = proj
    _, ckv_m, ckvT_m, _, kcat_m, _ = proj_meta
    lat = ckv.shape[-1]
    kcat_m, ckv_m, ckvT_m = kcat_m[0, :N_META], ckv_m[0, :N_META], ckvT_m[0, 0, :, :N_META]
    head_dim = wuvt.shape[1]
    return pl.pallas_call(
        functools.partial(_dsa_attn_kernel, meta_only=meta_only, topk=topk, head_dim=head_dim),
        grid=(bsz, nt),
        in_specs=[
            pl.BlockSpec((None, tt, d), lambda i, j: (i, j, 0)),
            pl.BlockSpec((None, IDX_HEADS, 3 * IDX_DIM, tt), lambda i, j: (i, 0, 0, j)),
            pl.BlockSpec((None, IDX_HEADS, tt), lambda i, j: (i, 0, j)),
            pl.BlockSpec((None, N_HEADS, lat, tt), lambda i, j: (i, 0, 0, j)),
            pl.BlockSpec((None, t, 3 * IDX_DIM), lambda i, j: (i, 0, 0),
                         pipeline_mode=pl.Buffered(1)),
            pl.BlockSpec((None, t, lat), lambda i, j: (i, 0, 0),
                         pipeline_mode=pl.Buffered(1)),
            pl.BlockSpec((None, nt, lat, tt), lambda i, j: (i, 0, 0, 0),
                         pipeline_mode=pl.Buffered(1)),
            _const_spec(kcat_m.shape),
            _const_spec(ckv_m.shape),
            _const_spec(ckvT_m.shape),
            _const_spec(wuvt.shape),
            _const_spec(w_out.shape),
            _const_spec(g.shape),
            _const_spec(b.shape),
        ],
        out_specs=pl.BlockSpec((None, tt, d), lambda i, j: (i, j, 0)),
        out_shape=jax.ShapeDtypeStruct((bsz, t, d), F32),
        scratch_shapes=[
            pltpu.VMEM((nt, tt, tt), F32),
            pltpu.VMEM((N_META, tt), F32),
            pltpu.VMEM((8, tt), F32),
            pltpu.VMEM((N_HEADS, lat, tt), F32),
            pltpu.VMEM((2, tt, tt), F32),
            pltpu.VMEM((2, tt, tt), F32),
            pltpu.VMEM((2, tt, tt), BF16),
            pltpu.VMEM((2, tt, tt), BF16),
            pltpu.VMEM((d, tt), F32),
        ],
        compiler_params=_cparams(2),
        name="dsa_attn_ln",
    )(x3, qiT, wT, qlatT, kcat, ckv, ckvT, kcat_m, ckv_m, ckvT_m, wuvt, w_out, g, b)


def _row(v):
    return v.reshape(1, -1)


def kernel(x, meta, a_w_in, a_w_group, a_scale, a_w_out, b_w_in, b_w_uk, b_w_uv, b_w_out,
           c_w_in, c_conv, c_w_out, ln_mix_g, ln_mix_b, ffn_w_gu, ffn_w_down, ln_ffn_g, ln_ffn_b):
    bsz, seq, d = x.shape
    assert seq % DSA_TILE == 0 and meta.shape[0] == N_META
    topk = min(TOPK_MAX, seq // 4)
    lat = d // KV_LATENT_DIV
    row_tile = 2 * FFN_SUB_ROWS if (bsz * seq) % (2 * FFN_SUB_ROWS) == 0 else FFN_SUB_ROWS
    mixer_tile = 2 * MIXER_SUB_ROWS if seq % (2 * MIXER_SUB_ROWS) == 0 else MIXER_SUB_ROWS

    hx = x
    hm = meta.astype(x.dtype)

    for i in range(DEPTH):
        kind, jj = i % 3, i // 3
        g, b = _row(ln_mix_g[i]), _row(ln_mix_b[i])
        if kind == 0:
            args = (a_w_in[jj].astype(BF16), a_w_group[jj].astype(BF16), _row(a_scale[jj]),
                    a_w_out[jj].astype(BF16), g, b)
            hx_new = _pool_layer(hx, hm, *args, tq=mixer_tile, is_meta=False)
            hm = _pool_layer(hm[None], hm, *args, tq=N_META, is_meta=True)[0]
            hx = hx_new
        elif kind == 1:
            w = b_w_in[jj]
            o = 0
            wq = w[:, o:o + d].astype(BF16); o += d
            wckv = w[:, o:o + lat].astype(BF16); o += lat
            wqi = w[:, o:o + IDX_HEADS * IDX_DIM]; o += IDX_HEADS * IDX_DIM
            wk = w[:, o:o + IDX_DIM]; o += IDX_DIM
            ww = w[:, o:o + IDX_HEADS]
            wqi3t = jnp.tile(wqi.T.reshape(IDX_HEADS, 1, IDX_DIM, d), (1, 3, 1, 1))
            wqi3t = wqi3t.reshape(IDX_HEADS * 3 * IDX_DIM, d).astype(BF16)
            wk3 = jnp.tile(wk, (1, 3)).astype(BF16)
            wwt = ww.T.astype(BF16)
            wukt = jnp.swapaxes(b_w_uk[jj], 1, 2).astype(BF16)
            wuvt = jnp.swapaxes(b_w_uv[jj], 1, 2).astype(BF16)
            wout = b_w_out[jj].astype(BF16)
            pw = (wq, wukt, wckv, wqi3t, wk3, wwt)
            hm_pad = jnp.pad(hm, ((0, DSA_TILE - N_META), (0, 0)))[None]
            proj_m = _dsa_proj(hm_pad, *pw)
            proj_x = _dsa_proj(hx, *pw)
            hx_new = _dsa_attn(hx, proj_x, proj_m, wuvt, wout, g, b, meta_only=False, topk=topk)
            hm = _dsa_attn(hm_pad, proj_m, proj_m, wuvt, wout, g, b, meta_only=True,
                           topk=topk)[0, :N_META]
            hx = hx_new
        else:
            args = (c_w_in[jj].astype(BF16), c_conv[jj], c_w_out[jj].astype(BF16), g, b)
            hx_new = _conv_layer(hx, hm, *args, tq=mixer_tile, is_meta=False)
            hm = _conv_layer(hm[None], hm, *args, tq=N_META, is_meta=True)[0]
            hx = hx_new

        fargs = (ffn_w_gu[i].astype(BF16), ffn_w_down[i].astype(BF16),
                 _row(ln_ffn_g[i]), _row(ln_ffn_b[i]))
        hx = _ffn(hx.reshape(bsz * seq, d), *fargs, tile=row_tile).reshape(bsz, seq, d)
        if i + 1 < DEPTH:
            hm = _ffn(hm, *fargs, tile=N_META)
    return hx
```

```python
import functools

import jax
import jax.numpy as jnp
from jax import lax
from jax.experimental import pallas as pl
from jax.experimental.pallas import tpu as pltpu

CHUNK = 64
N_META = 16
POOL_WINDOWS = (2, 4, 8, 16)
N_HEADS = 8
KV_LATENT_DIV = 4
IDX_HEADS = 8
IDX_DIM = 64
TOPK_MAX = 256
CONV_WIDTH = 3
DEPTH = 4
ALPHA = (2.0 * DEPTH) ** 0.25
LN_EPS = 1e-5

BF16 = jnp.bfloat16
F32 = jnp.float32
I32 = jnp.int32

VMEM_LIMIT_BYTES = 56 * 1024 * 1024
DSA_TILE = 512
I16_ROWS = 16
INT_MIN = -(2 ** 31)
FLT_MAX = 3.4028234663852886e38
FLT_TINY = 1.1754943508222875e-38
BRACKET_REL = 1e-6
MAX_SEARCH_ITERS = 40
PEEL_STEPS = 2
COUNT_CHAINS = 4
SOFTMAX_CHAINS = 4
NEG_INF = float("-inf")
MAX_FLOOR = -1e30


def _cparams(n_axes):
    return pltpu.CompilerParams(
        dimension_semantics=("arbitrary",) * n_axes,
        vmem_limit_bytes=VMEM_LIMIT_BYTES,
    )


def _const_spec(shape):
    nd = len(shape)
    return pl.BlockSpec(shape, lambda *_: (0,) * nd, pipeline_mode=pl.Buffered(1))


def _layer_norm(z, g, b):
    mu = jnp.mean(z, axis=-1, keepdims=True)
    zc = z - mu
    var = jnp.mean(zc * zc, axis=-1, keepdims=True)
    return zc * lax.rsqrt(var + LN_EPS) * g + b


def _dot(a, b):
    return jnp.dot(a, b, preferred_element_type=F32)


def _dot_nt(a, b):
    return lax.dot_general(a, b, (((1,), (1,)), ((), ())), preferred_element_type=F32)


FFN_SUB_ROWS = 512

def _ffn_kernel(h_ref, wgu_ref, wd_ref, g_ref, b_ref, o_ref, *, d_ff, ff_chunk):
    rows = h_ref.shape[0]
    sub = min(rows, FFN_SUB_ROWS)
    for r0 in range(0, rows, sub):
        h = h_ref[r0:r0 + sub, :]
        hb = h.astype(BF16)
        acc = ALPHA * h
        for c in range(d_ff // ff_chunk):
            lo = c * ff_chunk
            gate = _dot(hb, wgu_ref[:, lo:lo + ff_chunk])
            up = _dot(hb, wgu_ref[:, d_ff + lo:d_ff + lo + ff_chunk])
            act = (gate * jax.nn.sigmoid(gate) * up).astype(BF16)
            acc = acc + _dot(act, wd_ref[lo:lo + ff_chunk, :])
        o_ref[r0:r0 + sub, :] = _layer_norm(acc, g_ref[...], b_ref[...])


def _ffn(h2d, w_gu, w_down, g, b, *, tile):
    n, d = h2d.shape
    d_ff = w_down.shape[0]
    ff_chunk = 256 if d_ff % 256 == 0 else d_ff
    return pl.pallas_call(
        functools.partial(_ffn_kernel, d_ff=d_ff, ff_chunk=ff_chunk),
        grid=(n // tile,),
        in_specs=[
            pl.BlockSpec((tile, d), lambda i: (i, 0)),
            _const_spec(w_gu.shape),
            _const_spec(w_down.shape),
            _const_spec(g.shape),
            _const_spec(b.shape),
        ],
        out_specs=pl.BlockSpec((tile, d), lambda i: (i, 0)),
        out_shape=jax.ShapeDtypeStruct((n, d), F32),
        compiler_params=_cparams(1),
        name="ffn_ln",
    )(h2d, w_gu, w_down, g, b)


POOL_HALO = 16
POOL_PAD = 8
MIXER_SUB_ROWS = 256
MIXER_SUBS = 4


def _pool_kernel(x_ref, meta_ref, win_ref, wg_ref, sc_ref, wout_ref, g_ref, b_ref,
                 o_ref, ubuf, s2, s4, s8, *, tq, is_meta):
    j = pl.program_id(1)
    d = x_ref.shape[-1]
    gw = d // len(POOL_WINDOWS)
    sub = min(tq, MIXER_SUB_ROWS)
    pad, top = POOL_PAD, POOL_PAD + POOL_HALO

    ubuf[0:pad, :] = jnp.zeros((pad, d), F32)
    if is_meta:
        ubuf[pad:top, :] = jnp.zeros((POOL_HALO, d), F32)
    else:
        @pl.when(j == 0)
        def _():
            ubuf[pad:top, :] = _dot(meta_ref[...].astype(BF16), win_ref[...])

        @pl.when(j > 0)
        def _():
            ubuf[pad:top, :] = ubuf[pad + tq:top + tq, :]

    for si, r0 in enumerate(range(0, tq, sub)):
        x = x_ref[r0:r0 + sub, :]
        u = _dot(x.astype(BF16), win_ref[...])
        base = top + r0
        ubuf[base:base + sub, :] = u

        ext = sub + POOL_HALO
        e0 = base - POOL_HALO
        s2[si, 0:pad, :] = jnp.zeros((pad, d), F32)
        s2[si, pad:pad + ext, :] = ubuf[e0:e0 + ext, :] + ubuf[e0 - 1:e0 - 1 + ext, :]
        s4[si, 0:pad, :] = jnp.zeros((pad, d - gw), F32)
        s4[si, pad:pad + ext, :] = s2[si, pad:pad + ext, gw:] + s2[si, pad - 2:pad - 2 + ext, gw:]
        s8[si, :, :] = s4[si, pad:pad + ext, gw:] + s4[si, pad - 4:pad - 4 + ext, gw:]
        t0 = pad + POOL_HALO
        win_sums = (
            s2[si, t0:t0 + sub, 0:gw],
            s4[si, t0:t0 + sub, 0:gw],
            s8[si, POOL_HALO:POOL_HALO + sub, 0:gw],
            s8[si, POOL_HALO:POOL_HALO + sub, gw:] + s8[si, POOL_HALO - 8:POOL_HALO - 8 + sub, gw:],
        )

        ys = []
        for gi, w in enumerate(POOL_WINDOWS):
            c0 = gi * gw
            s = win_sums[gi]
            if is_meta:
                pos = lax.broadcasted_iota(I32, (sub, 1), 0) + 1
                cnt = jnp.minimum(pos, w).astype(F32)
                pooled = s / cnt
            else:
                pooled = s / float(w)
            dlt = (pooled - u[:, c0:c0 + gw]).astype(BF16)
            ys.append(_dot(dlt, wg_ref[gi]))
        y = jnp.concatenate(ys, axis=-1) * sc_ref[...]
        m = _dot(y.astype(BF16), wout_ref[...])
        o_ref[r0:r0 + sub, :] = _layer_norm(ALPHA * x + m, g_ref[...], b_ref[...])


def _pool_layer(x3, meta, w_in, w_group, scale, w_out, g, b, *, tq, is_meta):
    bsz, t, d = x3.shape
    assert POOL_WINDOWS == (2, 4, 8, 16)
    gw = d // len(POOL_WINDOWS)
    sub = min(tq, MIXER_SUB_ROWS)
    n_sub, ext = tq // sub, sub + POOL_HALO
    return pl.pallas_call(
        functools.partial(_pool_kernel, tq=tq, is_meta=is_meta),
        grid=(bsz, t // tq),
        in_specs=[
            pl.BlockSpec((None, tq, d), lambda i, j: (i, j, 0)),
            _const_spec(meta.shape),
            _const_spec(w_in.shape),
            _const_spec(w_group.shape),
            _const_spec(scale.shape),
            _const_spec(w_out.shape),
            _const_spec(g.shape),
            _const_spec(b.shape),
        ],
        out_specs=pl.BlockSpec((None, tq, d), lambda i, j: (i, j, 0)),
        out_shape=jax.ShapeDtypeStruct((bsz, t, d), F32),
        scratch_shapes=[
            pltpu.VMEM((POOL_PAD + POOL_HALO + tq, d), F32),
            pltpu.VMEM((n_sub, POOL_PAD + ext, d), F32),
            pltpu.VMEM((n_sub, POOL_PAD + ext, d - gw), F32),
            pltpu.VMEM((n_sub, ext, d - 2 * gw), F32),
        ],
        compiler_params=_cparams(2),
        name="pool_ln",
    )(x3, meta, w_in, w_group, scale, w_out, g, b)


CONV_HALO = 8


def _conv_kernel(x_ref, meta_ref, win_ref, cw_ref, wout_ref, g_ref, b_ref,
                 o_ref, zbuf, *, tq, is_meta):
    j = pl.program_id(1)
    d = x_ref.shape[-1]
    sub = min(tq, MIXER_SUB_ROWS)

    def gates(vb):
        bg = _dot(vb, win_ref[:, 0:d])
        cg = _dot(vb, win_ref[:, d:2 * d])
        hv = _dot(vb, win_ref[:, 2 * d:3 * d])
        return bg, cg * hv

    if is_meta:
        zbuf[0:CONV_HALO, :] = jnp.zeros((CONV_HALO, d), F32)
    else:
        @pl.when(j == 0)
        def _():
            _, zm = gates(meta_ref[...].astype(BF16))
            zbuf[0:CONV_HALO, :] = zm[N_META - CONV_HALO:N_META, :]

        @pl.when(j > 0)
        def _():
            zbuf[0:CONV_HALO, :] = zbuf[tq:tq + CONV_HALO, :]

    for r0 in range(0, tq, sub):
        x = x_ref[r0:r0 + sub, :]
        bg, z = gates(x.astype(BF16))
        base = CONV_HALO + r0
        zbuf[base:base + sub, :] = z
        conv = z * cw_ref[CONV_WIDTH - 1:CONV_WIDTH, :]
        for k in range(CONV_WIDTH - 1):
            sh = CONV_WIDTH - 1 - k
            conv = conv + zbuf[base - sh:base - sh + sub, :] * cw_ref[k:k + 1, :]
        m = _dot((bg * conv).astype(BF16), wout_ref[...])
        o_ref[r0:r0 + sub, :] = _layer_norm(ALPHA * x + m, g_ref[...], b_ref[...])


def _conv_layer(x3, meta, w_in, conv_w, w_out, g, b, *, tq, is_meta):
    bsz, t, d = x3.shape
    return pl.pallas_call(
        functools.partial(_conv_kernel, tq=tq, is_meta=is_meta),
        grid=(bsz, t // tq),
        in_specs=[
            pl.BlockSpec((None, tq, d), lambda i, j: (i, j, 0)),
            _const_spec(meta.shape),
            _const_spec(w_in.shape),
            _const_spec(conv_w.shape),
            _const_spec(w_out.shape),
            _const_spec(g.shape),
            _const_spec(b.shape),
        ],
        out_specs=pl.BlockSpec((None, tq, d), lambda i, j: (i, j, 0)),
        out_shape=jax.ShapeDtypeStruct((bsz, t, d), F32),
        scratch_shapes=[pltpu.VMEM((tq + CONV_HALO, d), F32)],
        compiler_params=_cparams(2),
        name="conv_ln",
    )(x3, meta, w_in, conv_w, w_out, g, b)


def _split_hi_lo(v):
    hi = v.astype(BF16)
    lo = (v - hi.astype(F32)).astype(BF16)
    return hi, lo


def _dsa_proj_kernel(x_ref, wq_ref, wukt_ref, wckv_ref, wqi3t_ref, wk3_ref, wwt_ref,
                     qlatT_ref, ckv_ref, ckvT_ref, qiT_ref, kcat_ref, wT_ref):
    xb = x_ref[...].astype(BF16)
    hd = wukt_ref.shape[2]
    q = _dot(xb, wq_ref[...])
    for h in range(N_HEADS):
        qh = q[:, h * hd:(h + 1) * hd].astype(BF16)
        qlatT_ref[h] = _dot_nt(wukt_ref[h], qh).astype(BF16)
    ckv = _dot(xb, wckv_ref[...])
    ckv_ref[...] = ckv.astype(BF16)
    ckvT_ref[...] = ckv.T.astype(BF16)

    qi3 = _dot_nt(wqi3t_ref[...], xb)
    hi, lo = _split_hi_lo(qi3)
    row = lax.broadcasted_iota(I32, qi3.shape, 0) % (3 * IDX_DIM)
    is_lo = (row >= IDX_DIM) & (row < 2 * IDX_DIM)
    qsel = jnp.where(is_lo, lo, hi)
    for h in range(IDX_HEADS):
        qiT_ref[h] = qsel[h * 3 * IDX_DIM:(h + 1) * 3 * IDX_DIM, :]
    k3 = _dot(xb, wk3_ref[...])
    khi, klo = _split_hi_lo(k3)
    col = lax.broadcasted_iota(I32, k3.shape, 1)
    kcat_ref[...] = jnp.where(col >= 2 * IDX_DIM, klo, khi)
    wT_ref[...] = _dot_nt(wwt_ref[...], xb)


def _dsa_proj(x3, wq, wukt, wckv, wqi3t, wk3, wwt):
    bsz, t, d = x3.shape
    tt = DSA_TILE
    nt = t // tt
    lat = wckv.shape[1]
    outs = pl.pallas_call(
        _dsa_proj_kernel,
        grid=(bsz, nt),
        in_specs=[
            pl.BlockSpec((None, tt, d), lambda i, j: (i, j, 0)),
            _const_spec(wq.shape),
            _const_spec(wukt.shape),
            _const_spec(wckv.shape),
            _const_spec(wqi3t.shape),
            _const_spec(wk3.shape),
            _const_spec(wwt.shape),
        ],
        out_specs=[
            pl.BlockSpec((None, N_HEADS, lat, tt), lambda i, j: (i, 0, 0, j)),
            pl.BlockSpec((None, tt, lat), lambda i, j: (i, j, 0)),
            pl.BlockSpec((None, None, lat, tt), lambda i, j: (i, j, 0, 0)),
            pl.BlockSpec((None, IDX_HEADS, 3 * IDX_DIM, tt), lambda i, j: (i, 0, 0, j)),
            pl.BlockSpec((None, tt, 3 * IDX_DIM), lambda i, j: (i, j, 0)),
            pl.BlockSpec((None, IDX_HEADS, tt), lambda i, j: (i, 0, j)),
        ],
        out_shape=[
            jax.ShapeDtypeStruct((bsz, N_HEADS, lat, t), BF16),
            jax.ShapeDtypeStruct((bsz, t, lat), BF16),
            jax.ShapeDtypeStruct((bsz, nt, lat, tt), BF16),
            jax.ShapeDtypeStruct((bsz, IDX_HEADS, 3 * IDX_DIM, t), BF16),
            jax.ShapeDtypeStruct((bsz, t, 3 * IDX_DIM), BF16),
            jax.ShapeDtypeStruct((bsz, IDX_HEADS, t), F32),
        ],
        compiler_params=_cparams(2),
        name="dsa_proj",
    )(x3, wq, wukt, wckv, wqi3t, wk3, wwt)
    return outs


def _dsa_attn_kernel(x_ref, qiT_ref, wT_ref, qlatT_ref, kcat_ref, ckv_ref, ckvT_ref,
                     kcat_m_ref, ckv_m_ref, ckvT_m_ref, wuvt_ref, wout_ref, g_ref, b_ref,
                     o_ref, scores, scores_m, thr_s, acc_s, lg_a, lg_b, p_a, p_b, outT,
                     *, meta_only, topk, head_dim):
    tt = DSA_TILE
    j = pl.program_id(1)
    nx = 0 if meta_only else j + 1

    def tile_rows(t):
        return pl.ds(pl.multiple_of(t * tt, tt), tt)

    def raw_scores(kc):
        acc = jnp.zeros((kc.shape[0], tt), F32)
        for h in range(IDX_HEADS):
            s = _dot(kc, qiT_ref[h])
            acc = acc + jnp.maximum(s, 0.0) * wT_ref[h:h + 1, :]
        return acc

    sm = raw_scores(kcat_m_ref[...])
    scores_m[...] = sm

    def score_body(t, carry):
        mx, mn = carry
        row = lax.broadcasted_iota(I32, (tt, tt), 0)
        col = lax.broadcasted_iota(I32, (tt, tt), 1)
        kchunk = t * (tt // CHUNK) + row // CHUNK
        qchunk = j * (tt // CHUNK) + col // CHUNK
        acc = raw_scores(kcat_ref[tile_rows(t), :])
        scores[t] = jnp.where(kchunk <= qchunk, acc, NEG_INF)
        return (jnp.maximum(mx, acc.max(axis=0, keepdims=True)),
                jnp.minimum(mn, acc.min(axis=0, keepdims=True)))

    mx, mn = lax.fori_loop(0, nx, score_body, (sm.max(axis=0, keepdims=True),
                                               sm.min(axis=0, keepdims=True)))

    def chained_reduce(part, combine, neutral):
        def body(t, accs):
            accs = list(accs)
            for i in range(tt // 8):
                c = i % COUNT_CHAINS
                accs[c] = combine(accs[c], part(scores[t, i * 8:(i + 1) * 8, :]))
            return tuple(accs)

        first = part(scores_m[0:8, :])
        fill = jnp.full((8, tt), neutral, first.dtype)
        accs = lax.fori_loop(0, nx, body,
                             (first, part(scores_m[8:N_META, :])) + (fill,) * (COUNT_CHAINS - 2))
        total = accs[0]
        for a in accs[1:]:
            total = combine(total, a)
        return total

    def chained_count(hits):
        return chained_reduce(hits, jnp.add, 0).sum(axis=0, keepdims=True)

    def count(thr, strict):
        return chained_count(lambda v: jnp.where((v > thr) if strict else (v >= thr), 1.0, 0.0))

    def max_below(bound):
        part = lambda v: jnp.where(v < bound, v, NEG_INF)
        return chained_reduce(part, jnp.maximum, NEG_INF).max(axis=0, keepdims=True)

    kf = float(topk)
    if meta_only:
        n_valid = jnp.full((1, tt), float(N_META), F32)
    else:
        qpos = j * tt + lax.broadcasted_iota(I32, (1, tt), 1)
        n_valid = (N_META + CHUNK * (qpos // CHUNK + 1)).astype(F32)
    all_sel = n_valid <= kf
    c0 = count(jnp.zeros((1, tt), F32), False)
    pos = c0 > kf
    one, zero = jnp.ones((1, tt), F32), jnp.zeros((1, tt), F32)
    done = jnp.where(all_sel | (c0 == kf), one, zero)
    tstar = jnp.where(all_sel, -FLT_MAX, zero)
    lo = jnp.where(pos, zero, mn - jnp.abs(mn) * BRACKET_REL - 2.0 * FLT_TINY)
    hi = jnp.where(pos, mx + jnp.abs(mx) * BRACKET_REL + 2.0 * FLT_TINY, zero)
    glo = jnp.where(pos, c0 - kf, n_valid - kf)
    ghi = jnp.where(pos, -kf, c0 - kf)

    def n_open(done, tie0, stuck, fhi):
        return jnp.sum(jnp.where(((done + tie0 + stuck) == 0.0) & (fhi < -float(PEEL_STEPS)), 1.0, 0.0))

    def search_cond(c):
        return jnp.logical_and(c[0] < MAX_SEARCH_ITERS, c[1] > 0.0)

    def search_step(c):
        it, _, lo, hi, glo, ghi, fhi, done, tie0, stuck, last, tstar = c
        active = (done + tie0 + stuck) == 0.0
        cand = lo + (hi - lo) * (glo / (glo - ghi))
        cand = jnp.where((cand > lo) & (cand < hi), cand, 0.5 * lo + 0.5 * hi)
        zero_probe = jnp.logical_and(it == 0, lo == 0.0)
        cand = jnp.where(zero_probe, FLT_TINY, cand)
        inside = (cand > lo) & (cand < hi)
        g = count(cand, False) - kf
        ok = active & inside
        hit = ok & (g == 0.0)
        up = ok & (g > 0.0)
        dn = ok & (g < 0.0)
        tie_now = dn & zero_probe
        glo_n = jnp.where(up, g, jnp.where(dn & (last < 0.0), 0.5 * glo, glo))
        ghi_n = jnp.where(dn, g, jnp.where(up & (last > 0.0), 0.5 * ghi, ghi))
        fhi = jnp.where(dn, g, fhi)
        done = jnp.where(hit, 1.0, done)
        tie0 = jnp.where(tie_now, 1.0, tie0)
        stuck = jnp.where(active & jnp.logical_not(inside), 1.0, stuck)
        return (it + 1, n_open(done, tie0, stuck, fhi), jnp.where(up, cand, lo),
                jnp.where(dn, cand, hi), glo_n, ghi_n, fhi, done, tie0, stuck,
                jnp.where(up, 1.0, jnp.where(dn, -1.0, last)), jnp.where(hit, cand, tstar))

    res = lax.while_loop(search_cond, search_step,
                         (jnp.int32(0), n_open(done, zero, zero, ghi), lo, hi, glo, ghi, ghi,
                          done, zero, zero, zero, tstar))
    hi, fhi, done, tie0, stuck, tstar = res[3], res[6], res[7], res[8], res[9], res[11]
    thr_s[0:1, :] = jnp.where(tie0 > 0.0, 0.0, tstar)
    thr_s[1:2, :] = tie0
    thr_s[2:3, :] = done + tie0

    peel = ((done + tie0 + stuck) == 0.0) & (fhi >= -float(PEEL_STEPS))

    @pl.when(jnp.sum(jnp.where(peel, 1.0, 0.0)) > 0.0)
    def _():
        t_peel = hi
        for i in range(PEEL_STEPS):
            t_peel = jnp.where(-fhi > float(i), max_below(t_peel), t_peel)
        good = peel & (count(t_peel, False) == kf)
        thr_s[0:1, :] = jnp.where(good, t_peel, thr_s[0:1, :])
        thr_s[2:3, :] = jnp.where(good, 1.0, thr_s[2:3, :])

    resolved = thr_s[2:3, :] > 0.0

    @pl.when(jnp.sum(jnp.where(resolved, 0.0, 1.0)) > 0.0)
    def _():
        def int_key(v):
            bits = pltpu.bitcast(jnp.where(v == 0.0, 0.0, v), I32)
            return jnp.where(bits < 0, bits ^ 0x7FFFFFFF, bits)

        def bisect(i, kth):
            cand = kth + lax.shift_left(jnp.int32(1), 31 - i)
            cnt = chained_count(lambda v: jnp.where(int_key(v) >= cand, 1, 0))
            return jnp.where(cnt >= topk, cand, kth)

        kth = lax.fori_loop(0, 32, bisect, jnp.full((1, tt), INT_MIN, I32))
        t_fb = pltpu.bitcast(jnp.where(kth < 0, kth ^ 0x7FFFFFFF, kth), F32)
        ties_fb = jnp.where(count(t_fb, False) > kf, 1.0, 0.0)
        thr_s[0:1, :] = jnp.where(resolved, thr_s[0:1, :], t_fb)
        thr_s[1:2, :] = jnp.where(resolved, thr_s[1:2, :], ties_fb)

    thr = thr_s[0:1, :]
    has_ties = jnp.sum(thr_s[1:2, :]) > 0.0

    @pl.when(jnp.logical_not(has_ties))
    def _():
        scores_m[...] = jnp.where(scores_m[...] >= thr, 0.0, NEG_INF)

        def body(t, carry):
            scores[t] = jnp.where(scores[t] >= thr, 0.0, NEG_INF)
            return carry

        lax.fori_loop(0, nx, body, 0)

    @pl.when(has_ties)
    def _():
        need = kf - count(thr, True)

        def select(v, before):
            tied = jnp.where(before < need, 0.0, NEG_INF)
            return jnp.where(v > thr, 0.0, jnp.where(v == thr, tied, NEG_INF))

        def strictly_lower(n):
            r = lax.broadcasted_iota(I32, (n, n), 0)
            c = lax.broadcasted_iota(I32, (n, n), 1)
            return jnp.where(c < r, 1.0, 0.0).astype(BF16)

        vm = scores_m[...]
        eqm = jnp.where(vm == thr, 1.0, 0.0)
        scores_m[...] = select(vm, _dot(strictly_lower(N_META), eqm.astype(BF16)))
        ltri = strictly_lower(tt)

        def body(t, taken):
            v = scores[t]
            eqf = jnp.where(v == thr, 1.0, 0.0)
            scores[t] = select(v, _dot(ltri, eqf.astype(BF16)) + taken)
            return taken + eqf.sum(axis=0, keepdims=True)

        lax.fori_loop(0, nx, body, eqm.sum(axis=0, keepdims=True))

    c_exp = (head_dim ** -0.5) * 1.4426950408889634

    bias_m = scores_m[...]
    ms, ls = [], []
    for h in range(N_HEADS):
        lg = _dot(ckv_m_ref[...], qlatT_ref[h]) + bias_m
        m = jnp.maximum(lg.max(axis=0, keepdims=True), MAX_FLOOR)
        p = jnp.exp2((lg - m) * c_exp)
        ms.append(m)
        ls.append(p.sum(axis=0, keepdims=True))
        acc_s[h] = _dot(ckvT_m_ref[...], p.astype(BF16))

    def tile_step(t, carry):
        m_all, l_all = carry
        kv = ckv_ref[tile_rows(t), :]
        kvT = ckvT_ref[t]
        ms, ls = [], []

        def tree(vals, op):
            while len(vals) > 1:
                vals = [op(a, b) for a, b in zip(vals[0::2], vals[1::2])]
            return vals[0]

        par = t & 1

        def biased_max(lg):
            accs = [None] * SOFTMAX_CHAINS
            for i in range(tt // 8):
                rows = slice(i * 8, (i + 1) * 8)
                x = lg[par, rows, :] + scores[t, rows, :]
                lg[par, rows, :] = x
                c = i % SOFTMAX_CHAINS
                accs[c] = x if accs[c] is None else jnp.maximum(accs[c], x)
            return tree(accs, jnp.maximum).max(axis=0, keepdims=True)

        def exp_chunks(lg, ps, m_new):
            accs = [None] * SOFTMAX_CHAINS
            for i in range(tt // I16_ROWS):
                rows = slice(i * I16_ROWS, (i + 1) * I16_ROWS)
                p = jnp.exp2((lg[par, rows, :] - m_new) * c_exp)
                ps[par, rows, :] = p.astype(BF16)
                c = i % SOFTMAX_CHAINS
                accs[c] = p if accs[c] is None else accs[c] + p
            return tree(accs, jnp.add).sum(axis=0, keepdims=True)

        lgs, pss = (lg_a, lg_b), (p_a, p_b)
        lgs[0][par] = _dot(kv, qlatT_ref[0])
        for h in range(N_HEADS):
            if h + 1 < N_HEADS:
                lgs[(h + 1) % 2][par] = _dot(kv, qlatT_ref[h + 1])
            lg, ps = lgs[h % 2], pss[h % 2]
            m_old = m_all[h:h + 1, :]
            m_new = jnp.maximum(m_old, biased_max(lg))
            a = jnp.exp2((m_old - m_new) * c_exp)
            l_tile = exp_chunks(lg, ps, m_new)
            ms.append(m_new)
            ls.append(a * l_all[h:h + 1, :] + l_tile)
            acc_s[h] = a * acc_s[h] + _dot(kvT, ps[par])
        return jnp.concatenate(ms, axis=0), jnp.concatenate(ls, axis=0)

    _, l_all = lax.fori_loop(0, nx, tile_step,
                             (jnp.concatenate(ms, axis=0), jnp.concatenate(ls, axis=0)))

    hd = wuvt_ref.shape[1]
    inv_l = 1.0 / l_all
    for h in range(N_HEADS):
        ctx = (acc_s[h] * inv_l[h:h + 1, :]).astype(BF16)
        outT[h * hd:(h + 1) * hd, :] = _dot(wuvt_ref[h], ctx)

    out = outT[...].T.astype(BF16)
    mix = _dot(out, wout_ref[...])
    o_ref[...] = _layer_norm(ALPHA * x_ref[...] + mix, g_ref[...], b_ref[...])


def _dsa_attn(x3, proj, proj_meta, wuvt, w_out, g, b, *, meta_only, topk):
    bsz, t, d = x3.shape
    tt = DSA_TILE
    nt = t // tt
    qlatT, ckv, ckvT, qiT, kcat, wT = proj
    _, ckv_m, ckvT_m, _, kcat_m, _ = proj_meta
    lat = ckv.shape[-1]
    kcat_m, ckv_m, ckvT_m = kcat_m[0, :N_META], ckv_m[0, :N_META], ckvT_m[0, 0, :, :N_META]
    head_dim = wuvt.shape[1]
    return pl.pallas_call(
        functools.partial(_dsa_attn_kernel, meta_only=meta_only, topk=topk, head_dim=head_dim),
        grid=(bsz, nt),
        in_specs=[
            pl.BlockSpec((None, tt, d), lambda i, j: (i, j, 0)),
            pl.BlockSpec((None, IDX_HEADS, 3 * IDX_DIM, tt), lambda i, j: (i, 0, 0, j)),
            pl.BlockSpec((None, IDX_HEADS, tt), lambda i, j: (i, 0, j)),
            pl.BlockSpec((None, N_HEADS, lat, tt), lambda i, j: (i, 0, 0, j)),
            pl.BlockSpec((None, t, 3 * IDX_DIM), lambda i, j: (i, 0, 0),
                         pipeline_mode=pl.Buffered(1)),
            pl.BlockSpec((None, t, lat), lambda i, j: (i, 0, 0),
                         pipeline_mode=pl.Buffered(1)),
            pl.BlockSpec((None, nt, lat, tt), lambda i, j: (i, 0, 0, 0),
                         pipeline_mode=pl.Buffered(1)),
            _const_spec(kcat_m.shape),
            _const_spec(ckv_m.shape),
            _const_spec(ckvT_m.shape),
            _const_spec(wuvt.shape),
            _const_spec(w_out.shape),
            _const_spec(g.shape),
            _const_spec(b.shape),
        ],
        out_specs=pl.BlockSpec((None, tt, d), lambda i, j: (i, j, 0)),
        out_shape=jax.ShapeDtypeStruct((bsz, t, d), F32),
        scratch_shapes=[
            pltpu.VMEM((nt, tt, tt), F32),
            pltpu.VMEM((N_META, tt), F32),
            pltpu.VMEM((8, tt), F32),
            pltpu.VMEM((N_HEADS, lat, tt), F32),
            pltpu.VMEM((2, tt, tt), F32),
            pltpu.VMEM((2, tt, tt), F32),
            pltpu.VMEM((2, tt, tt), BF16),
            pltpu.VMEM((2, tt, tt), BF16),
            pltpu.VMEM((d, tt), F32),
        ],
        compiler_params=_cparams(2),
        name="dsa_attn_ln",
    )(x3, qiT, wT, qlatT, kcat, ckv, ckvT, kcat_m, ckv_m, ckvT_m, wuvt, w_out, g, b)


def _row(v):
    return v.reshape(1, -1)


def kernel(x, meta, a_w_in, a_w_group, a_scale, a_w_out, b_w_in, b_w_uk, b_w_uv, b_w_out,
           c_w_in, c_conv, c_w_out, ln_mix_g, ln_mix_b, ffn_w_gu, ffn_w_down, ln_ffn_g, ln_ffn_b):
    bsz, seq, d = x.shape
    assert seq % DSA_TILE == 0 and meta.shape[0] == N_META
    topk = min(TOPK_MAX, seq // 4)
    lat = d // KV_LATENT_DIV
    row_tile = 2 * FFN_SUB_ROWS if (bsz * seq) % (2 * FFN_SUB_ROWS) == 0 else FFN_SUB_ROWS
    mixer_tile = MIXER_SUBS * MIXER_SUB_ROWS if seq % (MIXER_SUBS * MIXER_SUB_ROWS) == 0 else MIXER_SUB_ROWS

    hx = x
    hm = meta.astype(x.dtype)

    for i in range(DEPTH):
        kind, jj = i % 3, i // 3
        g, b = _row(ln_mix_g[i]), _row(ln_mix_b[i])
        if kind == 0:
            args = (a_w_in[jj].astype(BF16), a_w_group[jj].astype(BF16), _row(a_scale[jj]),
                    a_w_out[jj].astype(BF16), g, b)
            hx_new = _pool_layer(hx, hm, *args, tq=mixer_tile, is_meta=False)
            hm = _pool_layer(hm[None], hm, *args, tq=N_META, is_meta=True)[0]
            hx = hx_new
        elif kind == 1:
            w = b_w_in[jj]
            o = 0
            wq = w[:, o:o + d].astype(BF16); o += d
            wckv = w[:, o:o + lat].astype(BF16); o += lat
            wqi = w[:, o:o + IDX_HEADS * IDX_DIM]; o += IDX_HEADS * IDX_DIM
            wk = w[:, o:o + IDX_DIM]; o += IDX_DIM
            ww = w[:, o:o + IDX_HEADS]
            wqi3t = jnp.tile(wqi.T.reshape(IDX_HEADS, 1, IDX_DIM, d), (1, 3, 1, 1))
            wqi3t = wqi3t.reshape(IDX_HEADS * 3 * IDX_DIM, d).astype(BF16)
            wk3 = jnp.tile(wk, (1, 3)).astype(BF16)
            wwt = ww.T.astype(BF16)
            wukt = jnp.swapaxes(b_w_uk[jj], 1, 2).astype(BF16)
            wuvt = jnp.swapaxes(b_w_uv[jj], 1, 2).astype(BF16)
            wout = b_w_out[jj].astype(BF16)
            pw = (wq, wukt, wckv, wqi3t, wk3, wwt)
            hm_pad = jnp.pad(hm, ((0, DSA_TILE - N_META), (0, 0)))[None]
            proj_m = _dsa_proj(hm_pad, *pw)
            proj_x = _dsa_proj(hx, *pw)
            hx_new = _dsa_attn(hx, proj_x, proj_m, wuvt, wout, g, b, meta_only=False, topk=topk)
            hm = _dsa_attn(hm_pad, proj_m, proj_m, wuvt, wout, g, b, meta_only=True,
                           topk=topk)[0, :N_META]
            hx = hx_new
        else:
            args = (c_w_in[jj].astype(BF16), c_conv[jj], c_w_out[jj].astype(BF16), g, b)
            hx_new = _conv_layer(hx, hm, *args, tq=mixer_tile, is_meta=False)
            hm = _conv_layer(hm[None], hm, *args, tq=N_META, is_meta=True)[0]
            hx = hx_new

        fargs = (ffn_w_gu[i].astype(BF16), ffn_w_down[i].astype(BF16),
                 _row(ln_ffn_g[i]), _row(ln_ffn_b[i]))
        hx = _ffn(hx.reshape(bsz * seq, d), *fargs, tile=row_tile).reshape(bsz, seq, d)
        if i + 1 < DEPTH:
            hm = _ffn(hm, *fargs, tile=N_META)
    return hx
```
